```python
import math
import jax
import jax.numpy as jnp
from jax import lax
import numpy as np

D_MODEL = 2048
BATCH = 2
SEQ = 16384
DEPTH = 2

CTX_LEN = 256
GRID_W = 64
EPS = 1e-6

GMLP_GROUPS = 4
GMLP_GROUP_DIM = D_MODEL // 8
GMLP_WIDTH = GMLP_GROUPS * GMLP_GROUP_DIM
CHUNK = 128
DIFF_HEADS = 8
DIFF_QK_DIM = 64
DIFF_V_DIM = 2 * DIFF_QK_DIM
QK_WIDTH = DIFF_HEADS * 2 * DIFF_QK_DIM
V_WIDTH = DIFF_HEADS * DIFF_V_DIM
IN_WIDTH = 2 * GMLP_WIDTH + 2 * QK_WIDTH + V_WIDTH
MIX_WIDTH = GMLP_WIDTH + V_WIDTH
Q_BLOCK = 128
ROPE_THETA = 10000.0
POOL_WINDOWS = (2, 4, 8, 16)
POOL_GROUPS = 4
POOL_GROUP_DIM = D_MODEL // POOL_GROUPS
N_EXPERTS = 64
TOP_K = 8
EXPERT_DIM = D_MODEL // 4
SHARED_DIM = EXPERT_DIM
ROUTED_SCALE = 2.5
MOE_BLOCK = 128

N_EVEN = (DEPTH + 1) // 2
N_ODD = DEPTH // 2

kernel_name = 'hybrid_diffusion_gmlp_diffattn_pool_moe'


def rmsnorm(x, g):
    xf = x.astype(jnp.float32)
    y = xf * lax.rsqrt(jnp.mean(xf * xf, axis=-1, keepdims=True) + EPS)
    return (y * g.astype(jnp.float32)).astype(x.dtype)


def modulate(h, shift, scale):
    return h * (1 + scale) + shift


def axial_rope_tables(n):
    rows = n // GRID_W
    row = jnp.repeat(jnp.arange(rows, dtype=jnp.float32), GRID_W)
    col = jnp.tile(jnp.arange(GRID_W, dtype=jnp.float32), rows)
    half = DIFF_QK_DIM // 2
    inv = 1.0 / (ROPE_THETA ** (jnp.arange(0, half, 2, dtype=jnp.float32) / half))
    ang_r = row[:, None] * inv[None, :]
    ang_c = col[:, None] * inv[None, :]
    return (jnp.cos(ang_r), jnp.sin(ang_r), jnp.cos(ang_c), jnp.sin(ang_c))


def _rotate(x, cos, sin):
    x1, x2 = jnp.split(x, 2, axis=-1)
    cs = cos[:, None, :]
    sn = sin[:, None, :]
    return jnp.concatenate([x1 * cs - x2 * sn, x2 * cs + x1 * sn], axis=-1)


def apply_axial_rope(x, tables):
    cr, sr, cc, scn = tables
    xr, xc = jnp.split(x, 2, axis=-1)
    return jnp.concatenate([_rotate(xr, cr, sr), _rotate(xc, cc, scn)], axis=-1).astype(x.dtype)


def diff_attend(q1, q2, k1, k2, v, lam):
    scale = DIFF_QK_DIM ** -0.5
    s1 = jnp.einsum('bqhd,bkhd->bhqk', q1, k1).astype(jnp.float32) * scale
    s2 = jnp.einsum('bqhd,bkhd->bhqk', q2, k2).astype(jnp.float32) * scale
    a = jax.nn.softmax(s1, axis=-1) - lam * jax.nn.softmax(s2, axis=-1)
    return jnp.einsum('bhqk,bkhd->bqhd', a.astype(v.dtype), v)


def blocked_diff_attention(q1, q2, k1, k2, v, lam):
    B, n, H, dk = q1.shape
    nb = n // Q_BLOCK

    def blocks(t):
        return t.reshape(B, nb, Q_BLOCK, H, dk).transpose(1, 0, 2, 3, 4)

    def one(qs):
        return diff_attend(qs[0], qs[1], k1, k2, v, lam)

    o = lax.map(one, (blocks(q1), blocks(q2)))
    return o.transpose(1, 0, 2, 3, 4).reshape(B, n, H, v.shape[-1])


def gmlp_mix(uv, norm_g, ws, bs):
    B, n, _ = uv.shape
    uv = jax.nn.gelu(uv)
    u = uv[..., :GMLP_WIDTH]
    vv = uv[..., GMLP_WIDTH:].reshape(B, n // CHUNK, CHUNK, GMLP_GROUPS, GMLP_GROUP_DIM)
    vv = rmsnorm(vv, norm_g.reshape(GMLP_GROUPS, GMLP_GROUP_DIM))
    mixed = jnp.einsum('gpq,bnqgc->bnpgc', ws, vv) + bs.T[:, :, None]
    return u * mixed.reshape(B, n, GMLP_WIDTH)


def even_mixer(h, hc, w_in, w_out, gm_g, gm_ws, gm_bs, lq1, lk1, lq2, lk2, sub_g,
               lam_init, rope, need_ctx):
    B, n, _ = h.shape
    nc = hc.shape[1]
    dk = DIFF_QK_DIM
    q_start = 2 * GMLP_WIDTH
    k_start = q_start + QK_WIDTH
    v_start = k_start + QK_WIDTH
    lam = (jnp.exp(jnp.sum(lq1.astype(jnp.float32) * lk1.astype(jnp.float32)))
           - jnp.exp(jnp.sum(lq2.astype(jnp.float32) * lk2.astype(jnp.float32)))
           + lam_init)
    p = h @ w_in
    uv = p[..., :q_start]
    q = p[..., q_start:k_start].reshape(B, n, DIFF_HEADS, 2 * dk)
    k = p[..., k_start:v_start].reshape(B, n, DIFF_HEADS, 2 * dk)
    v = p[..., v_start:].reshape(B, n, DIFF_HEADS, DIFF_V_DIM)
    if need_ctx:
        pc = h.dtype.type(0) + hc @ w_in
        kvc = pc[..., k_start:]
    else:
        kvc = hc @ w_in[:, k_start:]
    kc = kvc[..., :QK_WIDTH].reshape(B, nc, DIFF_HEADS, 2 * dk)
    vc = kvc[..., QK_WIDTH:].reshape(B, nc, DIFF_HEADS, DIFF_V_DIM)
    q1 = apply_axial_rope(q[..., :dk], rope)
    q2 = apply_axial_rope(q[..., dk:], rope)
    k1 = jnp.concatenate([apply_axial_rope(k[..., :dk], rope), kc[..., :dk]], axis=1)
    k2 = jnp.concatenate([apply_axial_rope(k[..., dk:], rope), kc[..., dk:]], axis=1)
    v_all = jnp.concatenate([v, vc], axis=1)
    o = blocked_diff_attention(q1, q2, k1, k2, v_all, lam)
    o = (rmsnorm(o, sub_g) * (1.0 - lam_init)).reshape(B, n, V_WIDTH)
    y = jnp.concatenate([gmlp_mix(uv, gm_g, gm_ws, gm_bs), o], axis=-1) @ w_out
    if not need_ctx:
        return y, None
    qc = pc[..., q_start:k_start].reshape(B, nc, DIFF_HEADS, 2 * dk)
    oc = diff_attend(qc[..., :dk], qc[..., dk:], kc[..., :dk], kc[..., dk:], vc, lam)
    oc = (rmsnorm(oc, sub_g) * (1.0 - lam_init)).reshape(B, nc, V_WIDTH)
    yc = jnp.concatenate([gmlp_mix(pc[..., :q_start], gm_g, gm_ws, gm_bs), oc], axis=-1) @ w_out
    return y, yc


def pool_mixer(h, pool_w, pool_scale):
    B, n, D = h.shape
    hf = h.astype(jnp.float32)
    cs = jnp.concatenate([jnp.zeros((B, 1, D), jnp.float32), jnp.cumsum(hf, axis=1)], axis=1)
    t = jnp.arange(n)
    groups = []
    for gi, w in enumerate(POOL_WINDOWS):
        sl = slice(gi * POOL_GROUP_DIM, (gi + 1) * POOL_GROUP_DIM)
        lo = jnp.clip(t - w // 2, 0, n)
        hi = jnp.clip(t + w // 2, 0, n)
        cnt = (hi - lo).astype(jnp.float32)[None, :, None]
        groups.append((cs[:, hi, sl] - cs[:, lo, sl]) / cnt - hf[..., sl])
    pooled = jnp.stack(groups, axis=2).astype(h.dtype)
    y = jnp.einsum('bngc,gcd->bngd', pooled, pool_w).reshape(B, n, D)
    return y * pool_scale


def swiglu(x, wg, wu, wd):
    return (jax.nn.silu(x @ wg) * (x @ wu)) @ wd


def grouped_experts(x, idx, w, e_gate, e_up, e_down):
    T, D = x.shape
    A = T * TOP_K
    n_blocks = -(-A // MOE_BLOCK) + N_EXPERTS
    P = n_blocks * MOE_BLOCK
    flat_e = idx.reshape(-1)
    flat_tok = jnp.repeat(jnp.arange(T, dtype=jnp.int32), TOP_K)
    flat_w = w.reshape(-1)
    order = jnp.argsort(flat_e)
    se = flat_e[order]
    counts = jnp.bincount(flat_e, length=N_EXPERTS)
    padded = (counts + MOE_BLOCK - 1) // MOE_BLOCK * MOE_BLOCK
    start = jnp.cumsum(counts) - counts
    pend = jnp.cumsum(padded)
    pstart = pend - padded
    dest = pstart[se] + jnp.arange(A) - start[se]
    row_tok = jnp.full((P,), T, jnp.int32).at[dest].set(flat_tok[order])
    row_w = jnp.zeros((P,), x.dtype).at[dest].set(flat_w[order])
    block_e = jnp.clip(jnp.searchsorted(pend, jnp.arange(n_blocks) * MOE_BLOCK, side='right'),
                       0, N_EXPERTS - 1)
    x_pad = jnp.concatenate([x, jnp.zeros((1, D), x.dtype)], axis=0)

    def step(acc, blk):
        tok, rw, e = blk
        xb = x_pad[tok]
        yb = swiglu(xb, e_gate[e], e_up[e], e_down[e]) * rw[:, None]
        return acc.at[tok].add(yb), None

    acc, _ = lax.scan(step, jnp.zeros((T + 1, D), x.dtype),
                      (row_tok.reshape(n_blocks, MOE_BLOCK), row_w.reshape(n_blocks, MOE_BLOCK), block_e))
    return acc[:T]


def moe(h, router_w, router_b, e_gate, e_up, e_down, s_gate, s_up, s_down):
    shp = h.shape
    x = h.reshape(-1, shp[-1])
    scores = jax.nn.sigmoid(x.astype(jnp.float32) @ router_w.astype(jnp.float32))
    _, idx = lax.top_k(scores + router_b.astype(jnp.float32), TOP_K)
    w = jnp.take_along_axis(scores, idx, axis=-1)
    w = w / jnp.sum(w, axis=-1, keepdims=True) * ROUTED_SCALE
    routed = grouped_experts(x, idx, w.astype(x.dtype), e_gate, e_up, e_down)
    return (routed + swiglu(x, s_gate, s_up, s_down)).reshape(shp)


def setup_inputs(seed: int = 0) -> dict:
    key = jax.random.key(seed)
    ks = jax.random.split(key, 32)
    D = D_MODEL

    def nrm(k, shape, s):
        return jax.random.normal(k, shape, jnp.float32) * s

    def gain(k, shape):
        return 1.0 + 0.02 * jax.random.normal(k, shape, jnp.float32)

    return {
        'x': nrm(ks[0], (BATCH, SEQ, D), 1.0),
        'c': nrm(ks[1], (BATCH, D), 1.0),
        'ctx': nrm(ks[2], (BATCH, CTX_LEN, D), 1.0),
        'c_ctx': nrm(ks[3], (D,), 1.0),
        'ada_w': nrm(ks[4], (DEPTH, D, 6 * D), 0.5 * D ** -0.5),
        'ada_b': nrm(ks[5], (DEPTH, 6 * D), 0.02),
        'norm1_g': gain(ks[6], (DEPTH, D)),
        'norm2_g': gain(ks[7], (DEPTH, D)),
        'w_in': nrm(ks[8], (N_EVEN, D, IN_WIDTH), D ** -0.5),
        'w_out': nrm(ks[9], (N_EVEN, MIX_WIDTH, D), MIX_WIDTH ** -0.5),
        'gmlp_norm_g': gain(ks[10], (N_EVEN, GMLP_WIDTH)),
        'gmlp_ws': nrm(ks[11], (N_EVEN, GMLP_GROUPS, CHUNK, CHUNK), CHUNK ** -0.5),
        'gmlp_bs': gain(ks[12], (N_EVEN, GMLP_GROUPS, CHUNK)),
        'lam_q1': nrm(ks[13], (N_EVEN, DIFF_QK_DIM), 0.1),
        'lam_k1': nrm(ks[14], (N_EVEN, DIFF_QK_DIM), 0.1),
        'lam_q2': nrm(ks[15], (N_EVEN, DIFF_QK_DIM), 0.1),
        'lam_k2': nrm(ks[16], (N_EVEN, DIFF_QK_DIM), 0.1),
        'subln_g': gain(ks[17], (N_EVEN, DIFF_V_DIM)),
        'pool_w': nrm(ks[18], (N_ODD, POOL_GROUPS, POOL_GROUP_DIM, POOL_GROUP_DIM), POOL_GROUP_DIM ** -0.5),
        'pool_scale': gain(ks[19], (N_ODD, D)),
        'router_w': nrm(ks[20], (DEPTH, D, N_EXPERTS), D ** -0.5),
        'router_b': nrm(ks[21], (DEPTH, N_EXPERTS), 0.01),
        'exp_gate': nrm(ks[22], (DEPTH, N_EXPERTS, D, EXPERT_DIM), D ** -0.5),
        'exp_up': nrm(ks[23], (DEPTH, N_EXPERTS, D, EXPERT_DIM), D ** -0.5),
        'exp_down': nrm(ks[24], (DEPTH, N_EXPERTS, EXPERT_DIM, D), EXPERT_DIM ** -0.5),
        'sh_gate': nrm(ks[25], (DEPTH, D, SHARED_DIM), D ** -0.5),
        'sh_up': nrm(ks[26], (DEPTH, D, SHARED_DIM), D ** -0.5),
        'sh_down': nrm(ks[27], (DEPTH, SHARED_DIM, D), SHARED_DIM ** -0.5),
        'final_g': gain(ks[28], (D,)),
    }


def reference(x, c, ctx, c_ctx, ada_w, ada_b, norm1_g, norm2_g, w_in, w_out, gmlp_norm_g,
              gmlp_ws, gmlp_bs, lam_q1, lam_k1, lam_q2, lam_k2, subln_g, pool_w, pool_scale,
              router_w, router_b, exp_gate, exp_up, exp_down, sh_gate, sh_up, sh_down, final_g):
    B, n, D = x.shape
    n_ctx = ctx.shape[1]
    rope = axial_rope_tables(n)
    sc = jax.nn.silu(c)
    scc = jax.nn.silu(c_ctx)
    xc = ctx
    for i in range(DEPTH):
        even = i % 2 == 0
        j = i // 2
        need_ctx = i < DEPTH - 1
        mod = (sc @ ada_w[i] + ada_b[i]).reshape(B, 1, 6, D)
        sh1, s1, g1, sh2, s2, g2 = (mod[:, :, m] for m in range(6))
        h = modulate(rmsnorm(x, norm1_g[i]), sh1, s1)
        if need_ctx or even:
            mc = (scc @ ada_w[i] + ada_b[i]).reshape(1, 1, 6, D)
            csh1, cs1, cg1, csh2, cs2, cg2 = (mc[:, :, m] for m in range(6))
            hc = modulate(rmsnorm(xc, norm1_g[i]), csh1, cs1)
        if even:
            lam_init = 0.8 - 0.6 * math.exp(-0.3 * i)
            y, yc = even_mixer(h, hc, w_in[j], w_out[j], gmlp_norm_g[j], gmlp_ws[j], gmlp_bs[j],
                               lam_q1[j], lam_k1[j], lam_q2[j], lam_k2[j], subln_g[j],
                               lam_init, rope, need_ctx)
        else:
            y = pool_mixer(h, pool_w[j], pool_scale[j])
            yc = pool_mixer(hc, pool_w[j], pool_scale[j]) if need_ctx else None
        x = x + g1 * y
        h2 = modulate(rmsnorm(x, norm2_g[i]), sh2, s2)
        moe_params = (router_w[i], router_b[i], exp_gate[i], exp_up[i], exp_down[i],
                      sh_gate[i], sh_up[i], sh_down[i])
        if need_ctx:
            xc = xc + cg1 * yc
            hc2 = modulate(rmsnorm(xc, norm2_g[i]), csh2, cs2)
            out = moe(jnp.concatenate([hc2, h2], axis=1), *moe_params)
            xc = xc + cg2 * out[:, :n_ctx]
            y2 = out[:, n_ctx:]
        else:
            y2 = moe(h2, *moe_params)
        x = x + g2 * y2
    return rmsnorm(x, final_g)
```

```python
import functools
import math

import jax
import jax.numpy as jnp
from jax import lax
from jax.experimental import pallas as pl
from jax.experimental.pallas import tpu as pltpu

EPS = 1e-6
GRID_W = 64
ROPE_THETA = 10000.0
HEADS = 8
QK_DIM = 64
V_DIM = 128
V_ROWS = V_DIM + 8
GMLP_GROUPS = 4
CHUNK = 128
POOL_WINDOWS = (2, 4, 8, 16)
POOL_HALO = 8
N_EXPERTS = 64
TOP_K = 8
ROUTED_SCALE = 2.5
LANES = 128
EXPERT_ROWS = 256
VMEM_LIMIT = 56 * 1024 * 1024

BF16 = jnp.bfloat16
F32 = jnp.float32
U32 = jnp.uint32
I32 = jnp.int32


def _cparams(sem):
    return pltpu.CompilerParams(dimension_semantics=sem, vmem_limit_bytes=VMEM_LIMIT)


def _dot(a, b):
    return jnp.dot(a, b, preferred_element_type=F32)


def _pack_pair(lo, hi):
    ulo = pltpu.bitcast(lo.astype(BF16).astype(F32), U32)
    uhi = pltpu.bitcast(hi.astype(BF16).astype(F32), U32)
    return (ulo >> 16) | uhi


def _unpack_pair(u):
    lo = pltpu.bitcast(u << 16, F32)
    hi = pltpu.bitcast(u & jnp.uint32(0xFFFF0000), F32)
    return lo, hi


def _rms_mod(x, g, shift, scale1p):
    y = x * lax.rsqrt(jnp.mean(x * x, axis=-1, keepdims=True) + EPS)
    return y * g * scale1p + shift


def _sigmoid(x):
    return 1.0 / (1.0 + jnp.exp(-x))


def _ada_kernel(s_ref, w_ref, b_ref, o_ref):
    o_ref[0] = _dot(s_ref[...].astype(BF16), w_ref[0].astype(BF16)) + b_ref[0]


def _ada_mod(s8, ada_w, ada_b):
    L, D, W = ada_w.shape
    tn = 768
    return pl.pallas_call(
        _ada_kernel,
        out_shape=jax.ShapeDtypeStruct((L, 8, W), F32),
        grid=(L, W // tn),
        in_specs=[pl.BlockSpec((8, D), lambda l, j: (0, 0)),
                  pl.BlockSpec((1, D, tn), lambda l, j: (l, 0, j)),
                  pl.BlockSpec((1, 1, tn), lambda l, j: (l, 0, j))],
        out_specs=pl.BlockSpec((1, 8, tn), lambda l, j: (l, 0, j)),
        compiler_params=_cparams(("arbitrary", "arbitrary")),
        name="ada_mod",
    )(s8, ada_w, ada_b.reshape(L, 1, W))


def _proj_in_kernel(x_ref, g_ref, sh_ref, sc_ref, w_ref, cq_ref, sq_ref, ck_ref, sk_ref,
                    o_ref, h_ref, *, q_tile, k_tile):
    j = pl.program_id(2)

    @pl.when(j == 0)
    def _():
        h_ref[...] = _rms_mod(x_ref[0], g_ref[...], sh_ref[0], sc_ref[0]).astype(BF16)

    acc = _dot(h_ref[...], w_ref[...])
    tn = acc.shape[1]

    def rope(cos_ref, sin_ref):
        reps = tn // LANES
        cos = jnp.concatenate([cos_ref[...]] * reps, axis=1)
        sin = jnp.concatenate([sin_ref[...]] * reps, axis=1)
        lane = lax.broadcasted_iota(I32, acc.shape, 1)
        partner = jnp.where((lane & 16) == 0,
                            pltpu.roll(acc, tn - 16, axis=1),
                            pltpu.roll(acc, 16, axis=1))
        return acc * cos + partner * sin

    is_q = j == q_tile
    is_k = j == k_tile

    @pl.when(is_q)
    def _():
        o_ref[0] = rope(cq_ref, sq_ref).astype(o_ref.dtype)

    @pl.when(is_k)
    def _():
        o_ref[0] = rope(ck_ref, sk_ref).astype(o_ref.dtype)

    @pl.when(jnp.logical_not(jnp.logical_or(is_q, is_k)))
    def _():
        o_ref[0] = acc.astype(o_ref.dtype)


def _proj_in(x, g, shift, scale1p, w_bf, tabs, tm):
    B, N, D = x.shape
    W = w_bf.shape[1]
    tn = 1024
    q_tile, k_tile = 2, 3
    kern = functools.partial(_proj_in_kernel, q_tile=q_tile, k_tile=k_tile)
    vec = pl.BlockSpec((1, 1, D), lambda b, i, j: (b, 0, 0))
    tab = pl.BlockSpec((tm, LANES), lambda b, i, j: (i, 0))
    return pl.pallas_call(
        kern,
        out_shape=jax.ShapeDtypeStruct((B, N, W), BF16),
        grid=(B, N // tm, W // tn),
        in_specs=[pl.BlockSpec((1, tm, D), lambda b, i, j: (b, i, 0)),
                  pl.BlockSpec((1, D), lambda b, i, j: (0, 0)),
                  vec, vec,
                  pl.BlockSpec((D, tn), lambda b, i, j: (0, j)),
                  tab, tab, tab, tab],
        out_specs=pl.BlockSpec((1, tm, tn), lambda b, i, j: (b, i, j)),
        scratch_shapes=[pltpu.VMEM((tm, D), BF16)],
        compiler_params=_cparams(("arbitrary", "arbitrary", "arbitrary")),
        name="proj_in",
    )(x, g, shift, scale1p, w_bf, *tabs)


def _gelu_tanh(x):
    return 0.5 * x * (1.0 + jnp.tanh(math.sqrt(2.0 / math.pi) * (x + 0.044715 * (x * x * x))))


def _gmlp_kernel(uv_ref, ng_ref, ws_ref, bs_ref, o_ref, *, chunks):
    gw = o_ref.shape[2]
    gd = gw // GMLP_GROUPS
    for c in range(chunks):
        rows = pl.ds(c * CHUNK, CHUNK)
        for g in range(GMLP_GROUPS):
            u = _gelu_tanh(uv_ref[0, rows, g * gd:(g + 1) * gd].astype(F32))
            v = _gelu_tanh(uv_ref[0, rows, gw + g * gd:gw + (g + 1) * gd].astype(F32))
            vn = v * lax.rsqrt(jnp.mean(v * v, axis=-1, keepdims=True) + EPS) * ng_ref[:, g * gd:(g + 1) * gd]
            mixed = _dot(ws_ref[g], vn.astype(BF16)) + bs_ref[g]
            o_ref[0, rows, g * gd:(g + 1) * gd] = (u * mixed).astype(o_ref.dtype)


def _gmlp(p, norm_g, ws_bf, bs_b, tg):
    B, N, _ = p.shape
    gw = norm_g.shape[1]
    gd = gw // GMLP_GROUPS
    kern = functools.partial(_gmlp_kernel, chunks=tg // CHUNK)
    return pl.pallas_call(
        kern,
        out_shape=jax.ShapeDtypeStruct((B, N, gw), BF16),
        grid=(B, N // tg),
        in_specs=[pl.BlockSpec((1, tg, 2 * gw), lambda b, i: (b, i, 0)),
                  pl.BlockSpec((1, gw), lambda b, i: (0, 0)),
                  pl.BlockSpec((GMLP_GROUPS, CHUNK, CHUNK), lambda b, i: (0, 0, 0)),
                  pl.BlockSpec((GMLP_GROUPS, CHUNK, gd), lambda b, i: (0, 0, 0))],
        out_specs=pl.BlockSpec((1, tg, gw), lambda b, i: (b, i, 0)),
        compiler_params=_cparams(("arbitrary", "arbitrary")),
        name="gmlp",
    )(p, norm_g, ws_bf, bs_b)


def _attn_kernel(lam_ref, q_ref, k_ref, vt_ref, g_ref, o_ref, *, tk, out_scale):
    tq = q_ref.shape[1]
    nk = k_ref.shape[1]
    q = q_ref[0]
    lane = lax.broadcasted_iota(I32, q.shape, 1)
    zero = jnp.zeros_like(q)
    qa = jnp.where(lane < QK_DIM, q, zero)
    qb = jnp.where(lane >= QK_DIM, q, zero)
    nt_dims = (((1,), (1,)), ((), ()))

    def one_map(qm, k, vt, m, acc):
        s = lax.dot_general(k, qm, nt_dims, preferred_element_type=F32)
        m_new = jnp.maximum(m, jnp.max(s, axis=0, keepdims=True))
        alpha = jnp.exp(m - m_new)
        p = jnp.exp(s - m_new).astype(BF16)
        return m_new, alpha * acc + _dot(vt, p)

    def step(t, carry):
        m1, a1, m2, a2 = carry
        start = pl.multiple_of(t * tk, tk)
        k = k_ref[0, pl.ds(start, tk), :]
        vt = vt_ref[0, 0, :, pl.ds(start, tk)]
        m1, a1 = one_map(qa, k, vt, m1, a1)
        m2, a2 = one_map(qb, k, vt, m2, a2)
        return m1, a1, m2, a2

    m0 = jnp.full((1, tq), -jnp.inf, F32)
    a0 = jnp.zeros((V_ROWS, tq), F32)
    _, a1, _, a2 = lax.fori_loop(0, nk // tk, step, (m0, a0, m0, a0))
    lam = lam_ref[0]
    o = a1[:V_DIM] / a1[V_DIM:V_DIM + 1] - lam * (a2[:V_DIM] / a2[V_DIM:V_DIM + 1])
    o = o * lax.rsqrt(jnp.mean(o * o, axis=0, keepdims=True) + EPS)
    reps = tq // LANES
    gain = jnp.concatenate([g_ref[...]] * reps, axis=1) if reps > 1 else g_ref[...]
    o_ref[0] = (o * gain * out_scale).T.astype(o_ref.dtype)


def _attention(lam, q_src, q_col0, k_all, vt, g_b, nq, tq, out_scale):
    B, nk, _ = k_all.shape
    tk = 1280 if nk % 1280 == 0 else 256
    assert nk % tk == 0 and nq % tq == 0
    kern = functools.partial(_attn_kernel, tk=tk, out_scale=out_scale)
    return pl.pallas_call(
        kern,
        out_shape=jax.ShapeDtypeStruct((B, nq, HEADS * V_DIM), BF16),
        grid=(B, HEADS, nq // tq),
        in_specs=[pl.BlockSpec(memory_space=pltpu.SMEM),
                  pl.BlockSpec((1, tq, LANES), lambda b, h, i: (b, i, q_col0 + h)),
                  pl.BlockSpec((1, nk, LANES), lambda b, h, i: (b, 0, h)),
                  pl.BlockSpec((1, 1, V_ROWS, nk), lambda b, h, i: (b, h, 0, 0)),
                  pl.BlockSpec((V_DIM, LANES), lambda b, h, i: (0, 0))],
        out_specs=pl.BlockSpec((1, tq, LANES), lambda b, h, i: (b, i, h)),
        compiler_params=_cparams(("arbitrary", "arbitrary", "arbitrary")),
        name="diff_attn",
    )(lam, q_src, k_all, vt, g_b)


def _mixer_epilogue(x, y, g1, n2g, sh2, sc2, rw, x1_ref, hp_ref, lg_ref):
    x1 = x + g1 * y
    x1_ref[0] = x1
    h2 = _rms_mod(x1, n2g, sh2, sc2)
    half = h2.shape[1] // 2
    hp_ref[0] = _pack_pair(h2[:, :half], h2[:, half:])
    lg_ref[0] = _dot(h2, rw)


def _proj_out_kernel(gm_ref, o_ref, w_ref, x_ref, g1_ref, n2g_ref, sh2_ref, sc2_ref, rw_ref,
                     x1_ref, hp_ref, lg_ref):
    gw = gm_ref.shape[2]
    y = _dot(gm_ref[0], w_ref[:gw, :]) + _dot(o_ref[0], w_ref[gw:, :])
    _mixer_epilogue(x_ref[0], y, g1_ref[0], n2g_ref[...], sh2_ref[0], sc2_ref[0], rw_ref[...],
                    x1_ref, hp_ref, lg_ref)


def _epilogue_out(B, N, D, tm):
    shapes = (jax.ShapeDtypeStruct((B, N, D), F32),
              jax.ShapeDtypeStruct((B, N, D // 2), U32),
              jax.ShapeDtypeStruct((B, N, LANES), F32))
    specs = (pl.BlockSpec((1, tm, D), lambda b, i: (b, i, 0)),
             pl.BlockSpec((1, tm, D // 2), lambda b, i: (b, i, 0)),
             pl.BlockSpec((1, tm, LANES), lambda b, i: (b, i, 0)))
    return shapes, specs


def _proj_out(gm, o, w_bf, x, g1, n2g, sh2, sc2, rw_pad, tm):
    B, N, D = x.shape
    gw = gm.shape[2]
    vec = pl.BlockSpec((1, 1, D), lambda b, i: (b, 0, 0))
    shapes, specs = _epilogue_out(B, N, D, tm)
    return pl.pallas_call(
        _proj_out_kernel,
        out_shape=shapes,
        grid=(B, N // tm),
        in_specs=[pl.BlockSpec((1, tm, gw), lambda b, i: (b, i, 0)),
                  pl.BlockSpec((1, tm, o.shape[2]), lambda b, i: (b, i, 0)),
                  pl.BlockSpec(w_bf.shape, lambda b, i: (0, 0)),
                  pl.BlockSpec((1, tm, D), lambda b, i: (b, i, 0)),
                  vec,
                  pl.BlockSpec((1, D), lambda b, i: (0, 0)),
                  vec, vec,
                  pl.BlockSpec((D, LANES), lambda b, i: (0, 0))],
        out_specs=specs,
        compiler_params=_cparams(("arbitrary", "arbitrary")),
        name="proj_out",
    )(gm, o, w_bf, x, g1, n2g, sh2, sc2, rw_pad)


def _pool_kernel(x_ref, xp_ref, xn_ref, n1g_ref, sh1_ref, sc1_ref, pw_ref, ps_ref,
                 g1_ref, n2g_ref, sh2_ref, sc2_ref, rw_ref, x1_ref, hp_ref, lg_ref, y_ref, *, seq):
    i = pl.program_id(1)
    last = pl.num_programs(1) - 1
    x = x_ref[0]
    tm, D = x.shape
    n1g, sh1, sc1 = n1g_ref[...], sh1_ref[0], sc1_ref[0]
    hc = _rms_mod(x, n1g, sh1, sc1)
    hp = _rms_mod(xp_ref[0], n1g, sh1, sc1) * (i > 0).astype(F32)
    hn = _rms_mod(xn_ref[0], n1g, sh1, sc1) * (i < last).astype(F32)
    he = jnp.concatenate([hp, hc, hn], axis=0).astype(BF16)
    groups = len(POOL_WINDOWS)
    gd = D // groups
    row = lax.broadcasted_iota(I32, (tm, tm + 2 * POOL_HALO), 0)
    col = lax.broadcasted_iota(I32, (tm, tm + 2 * POOL_HALO), 1)
    tglob = i * tm + lax.broadcasted_iota(I32, (tm, 1), 0)
    for g, w in enumerate(POOL_WINDOWS):
        lo = row + (POOL_HALO - w // 2)
        band = jnp.where(jnp.logical_and(col >= lo, col < lo + w), 1.0, 0.0).astype(BF16)
        s = _dot(band, he[:, g * gd:(g + 1) * gd])
        cnt = jnp.minimum(tglob + w // 2, seq) - jnp.maximum(tglob - w // 2, 0)
        pooled = s / cnt.astype(F32) - hc[:, g * gd:(g + 1) * gd]
        y_ref[:, g * gd:(g + 1) * gd] = _dot(pooled.astype(BF16), pw_ref[g])
    y = y_ref[...] * ps_ref[...]
    _mixer_epilogue(x, y, g1_ref[0], n2g_ref[...], sh2_ref[0], sc2_ref[0], rw_ref[...],
                    x1_ref, hp_ref, lg_ref)


def _pool_mixer(x, n1g, sh1, sc1, pw_bf, ps, g1, n2g, sh2, sc2, rw_pad, tm):
    B, N, D = x.shape
    gd = D // len(POOL_WINDOWS)
    hb = tm // POOL_HALO
    nhb = N // POOL_HALO
    vec = pl.BlockSpec((1, 1, D), lambda b, i: (b, 0, 0))
    row = pl.BlockSpec((1, D), lambda b, i: (0, 0))
    shapes, specs = _epilogue_out(B, N, D, tm)
    kern = functools.partial(_pool_kernel, seq=N)
    return pl.pallas_call(
        kern,
        out_shape=shapes,
        grid=(B, N // tm),
        in_specs=[pl.BlockSpec((1, tm, D), lambda b, i: (b, i, 0)),
                  pl.BlockSpec((1, POOL_HALO, D), lambda b, i: (b, jnp.maximum(i * hb - 1, 0), 0)),
                  pl.BlockSpec((1, POOL_HALO, D), lambda b, i: (b, jnp.minimum((i + 1) * hb, nhb - 1), 0)),
                  row, vec, vec,
                  pl.BlockSpec((len(POOL_WINDOWS), gd, gd), lambda b, i: (0, 0, 0)),
                  row, vec, row, vec, vec,
                  pl.BlockSpec((D, LANES), lambda b, i: (0, 0))],
        out_specs=specs,
        scratch_shapes=[pltpu.VMEM((tm, D), F32)],
        compiler_params=_cparams(("arbitrary", "arbitrary")),
        name="pool_mixer",
    )(x, x, x, n1g, sh1, sc1, pw_bf, ps, g1, n2g, sh2, sc2, rw_pad)


def _route_kernel(lg_ref, b_ref, idx_ref, w_ref, rank_ref, cnt_ref, carry_ref):
    @pl.when(pl.program_id(0) == 0)
    def _():
        carry_ref[...] = jnp.zeros_like(carry_ref)

    logits = lg_ref[...]
    tr = logits.shape[0]
    scores = _sigmoid(logits)
    sel = scores + b_ref[...]
    lane = lax.broadcasted_iota(I32, (tr, LANES), 1)
    picked = jnp.zeros((tr, LANES), F32)
    onehots, wk = [], []
    for _ in range(TOP_K):
        mx = jnp.max(sel, axis=1, keepdims=True)
        first = jnp.min(jnp.where(sel == mx, lane, LANES), axis=1, keepdims=True)
        oh = lane == first
        onehots.append(oh)
        wk.append(jnp.sum(jnp.where(oh, scores, 0.0), axis=1, keepdims=True))
        sel = jnp.where(oh, -jnp.inf, sel)
        picked = jnp.where(oh, 1.0, picked)
    wsum = wk[0]
    for v in wk[1:]:
        wsum = wsum + v
    r = lax.broadcasted_iota(I32, (tr, tr), 0)
    c = lax.broadcasted_iota(I32, (tr, tr), 1)
    below = jnp.where(c < r, 1.0, 0.0).astype(BF16)
    before = _dot(below, picked.astype(BF16)) + carry_ref[...]
    idx_o = jnp.zeros((tr, LANES), I32)
    w_o = jnp.zeros((tr, LANES), F32)
    rank_o = jnp.zeros((tr, LANES), I32)
    for k in range(TOP_K):
        oh = onehots[k]
        first = jnp.sum(jnp.where(oh, lane, 0), axis=1, keepdims=True)
        rk = jnp.sum(jnp.where(oh, before, 0.0), axis=1, keepdims=True).astype(I32)
        idx_o = jnp.where(lane == k, first, idx_o)
        w_o = jnp.where(lane == k, wk[k] / wsum * ROUTED_SCALE, w_o)
        rank_o = jnp.where(lane == k, rk, rank_o)
    idx_ref[...] = idx_o
    w_ref[...] = w_o
    rank_ref[...] = rank_o
    carry_ref[...] = carry_ref[...] + jnp.sum(picked, axis=0, keepdims=True)
    cnt_ref[...] = carry_ref[...]


def _route(logits, bias_pad, tr):
    T = logits.shape[0]
    blk = pl.BlockSpec((tr, LANES), lambda i: (i, 0))
    one = pl.BlockSpec((1, LANES), lambda i: (0, 0))
    return pl.pallas_call(
        _route_kernel,
        out_shape=(jax.ShapeDtypeStruct((T, LANES), I32),
                   jax.ShapeDtypeStruct((T, LANES), F32),
                   jax.ShapeDtypeStruct((T, LANES), I32),
                   jax.ShapeDtypeStruct((1, LANES), F32)),
        grid=(T // tr,),
        in_specs=[blk, one],
        out_specs=(blk, blk, blk, one),
        scratch_shapes=[pltpu.VMEM((1, LANES), F32)],
        compiler_params=_cparams(("arbitrary",)),
        name="route",
    )(logits, bias_pad)


def _dispatch_kernel(dest_ref, h_ref, xs_in_ref, xs_ref, sem):
    del xs_in_ref
    td = h_ref.shape[0]

    def row_copy(r, k):
        d = dest_ref[0, 0, r * TOP_K + k]
        return pltpu.make_async_copy(h_ref.at[pl.ds(r, 1), :], xs_ref.at[pl.ds(d, 1), :], sem)

    def issue(r, _):
        for k in range(TOP_K):
            row_copy(r, k).start()
        return 0

    lax.fori_loop(0, td, issue, 0)

    def drain(r, _):
        for k in range(TOP_K):
            row_copy(r, k).wait()
        return 0

    lax.fori_loop(0, td, drain, 0)


def _dispatch(dest3, hp, xs_zero, td):
    T, W = hp.shape
    return pl.pallas_call(
        _dispatch_kernel,
        out_shape=jax.ShapeDtypeStruct(xs_zero.shape, U32),
        grid=(T // td,),
        in_specs=[pl.BlockSpec((1, 1, td * TOP_K), lambda i: (i, 0, 0), memory_space=pltpu.SMEM),
                  pl.BlockSpec((td, W), lambda i: (i, 0)),
                  pl.BlockSpec(memory_space=pl.ANY)],
        out_specs=pl.BlockSpec(memory_space=pl.ANY),
        scratch_shapes=[pltpu.SemaphoreType.DMA(())],
        input_output_aliases={2: 0},
        compiler_params=_cparams(("arbitrary",)),
        name="dispatch",
    )(dest3, hp, xs_zero)


def _experts_kernel(be_ref, bv_ref, xs_ref, wg_ref, wu_ref, wd_ref, ys_ref, wgb, wub, wdb):
    i = pl.program_id(0)
    valid = bv_ref[i] != 0
    fresh = jnp.logical_or(i == 0, be_ref[i] != be_ref[jnp.maximum(i - 1, 0)])

    @pl.when(jnp.logical_and(valid, fresh))
    def _():
        wgb[...] = wg_ref[0].astype(BF16)
        wub[...] = wu_ref[0].astype(BF16)
        wdb[...] = wd_ref[0].astype(BF16)

    @pl.when(valid)
    def _():
        lo, hi = _unpack_pair(xs_ref[...])
        lo, hi = lo.astype(BF16), hi.astype(BF16)
        half = lo.shape[1]
        g = _dot(lo, wgb[:half, :]) + _dot(hi, wgb[half:, :])
        u = _dot(lo, wub[:half, :]) + _dot(hi, wub[half:, :])
        a = (g * _sigmoid(g) * u).astype(BF16)
        y = _dot(a, wdb[...])
        ys_ref[...] = _pack_pair(y[:, :half], y[:, half:])

    @pl.when(jnp.logical_not(valid))
    def _():
        ys_ref[...] = jnp.zeros_like(ys_ref)


def _experts(block_e, block_valid, xs, wg, wu, wd):
    P, W = xs.shape
    _, D, ED = wg.shape
    R = EXPERT_ROWS
    grid_spec = pltpu.PrefetchScalarGridSpec(
        num_scalar_prefetch=2,
        grid=(P // R,),
        in_specs=[pl.BlockSpec((R, W), lambda i, be, bv: (i, 0)),
                  pl.BlockSpec((1, D, ED), lambda i, be, bv: (be[i], 0, 0)),
                  pl.BlockSpec((1, D, ED), lambda i, be, bv: (be[i], 0, 0)),
                  pl.BlockSpec((1, ED, D), lambda i, be, bv: (be[i], 0, 0))],
        out_specs=pl.BlockSpec((R, W), lambda i, be, bv: (i, 0)),
        scratch_shapes=[pltpu.VMEM((D, ED), BF16), pltpu.VMEM((D, ED), BF16), pltpu.VMEM((ED, D), BF16)],
    )
    return pl.pallas_call(
        _experts_kernel,
        out_shape=jax.ShapeDtypeStruct((P, W), U32),
        grid_spec=grid_spec,
        compiler_params=_cparams(("arbitrary",)),
        name="experts",
    )(block_e, block_valid, xs, wg, wu, wd)


def _combine_kernel(dest_ref, ys_ref, wts_ref, hp_ref, x_ref, g2_ref, sg_ref, su_ref, sd_ref, fg_ref,
                    o_ref, buf, sem, *, final_norm):
    tm = hp_ref.shape[0]

    def row_copy(r, k):
        d = dest_ref[0, 0, r * TOP_K + k]
        return pltpu.make_async_copy(ys_ref.at[pl.ds(d, 1), :], buf.at[k, pl.ds(r, 1), :], sem)

    def issue(r, _):
        for k in range(TOP_K):
            row_copy(r, k).start()
        return 0

    lax.fori_loop(0, tm, issue, 0)

    lo, hi = _unpack_pair(hp_ref[...])
    lo, hi = lo.astype(BF16), hi.astype(BF16)
    half = lo.shape[1]
    g = _dot(lo, sg_ref[:half, :]) + _dot(hi, sg_ref[half:, :])
    u = _dot(lo, su_ref[:half, :]) + _dot(hi, su_ref[half:, :])
    shared = _dot((g * _sigmoid(g) * u).astype(BF16), sd_ref[...])

    def drain(r, _):
        for k in range(TOP_K):
            row_copy(r, k).wait()
        return 0

    lax.fori_loop(0, tm, drain, 0)

    wts = wts_ref[...]
    acc_lo = shared[:, :half]
    acc_hi = shared[:, half:]
    for k in range(TOP_K):
        ylo, yhi = _unpack_pair(buf[k])
        wk = wts[:, k:k + 1]
        acc_lo = acc_lo + wk * ylo
        acc_hi = acc_hi + wk * yhi
    g2 = g2_ref[0]
    x = x_ref[...]
    out_lo = x[:, :half] + g2[:, :half] * acc_lo
    out_hi = x[:, half:] + g2[:, half:] * acc_hi
    if final_norm:
        ms = (jnp.sum(out_lo * out_lo, axis=-1, keepdims=True)
              + jnp.sum(out_hi * out_hi, axis=-1, keepdims=True)) / (2 * half)
        inv = lax.rsqrt(ms + EPS)
        fg = fg_ref[...]
        out_lo = out_lo * inv * fg[:, :half]
        out_hi = out_hi * inv * fg[:, half:]
    o_ref[:, :half] = out_lo
    o_ref[:, half:] = out_hi


def _combine(dest3, ys, wts, hp, x1, g2, sg_bf, su_bf, sd_bf, fg, tm, tiles_per_batch, final_norm):
    T, D = x1.shape
    W = D // 2
    ED = sg_bf.shape[1]
    kern = functools.partial(_combine_kernel, final_norm=final_norm)
    return pl.pallas_call(
        kern,
        out_shape=jax.ShapeDtypeStruct((T, D), F32),
        grid=(T // tm,),
        in_specs=[pl.BlockSpec((1, 1, tm * TOP_K), lambda i: (i, 0, 0), memory_space=pltpu.SMEM),
                  pl.BlockSpec(memory_space=pl.ANY),
                  pl.BlockSpec((tm, LANES), lambda i: (i, 0)),
                  pl.BlockSpec((tm, W), lambda i: (i, 0)),
                  pl.BlockSpec((tm, D), lambda i: (i, 0)),
                  pl.BlockSpec((1, 1, D), lambda i: (i // tiles_per_batch, 0, 0)),
                  pl.BlockSpec((D, ED), lambda i: (0, 0)),
                  pl.BlockSpec((D, ED), lambda i: (0, 0)),
                  pl.BlockSpec((ED, D), lambda i: (0, 0)),
                  pl.BlockSpec((1, D), lambda i: (0, 0))],
        out_specs=pl.BlockSpec((tm, D), lambda i: (i, 0)),
        scratch_shapes=[pltpu.VMEM((TOP_K, tm, W), U32), pltpu.SemaphoreType.DMA(())],
        compiler_params=_cparams(("arbitrary",)),
        name="combine",
    )(dest3, ys, wts, hp, x1, g2, sg_bf, su_bf, sd_bf, fg)


def _moe(hp, logits, x1, g2, n_out, tiles_per_batch, tm, router_b, wg, wu, wd, sg, su, sd, fg, final_norm):
    T = hp.shape[0]
    R = EXPERT_ROWS
    bias_pad = jnp.full((1, LANES), -1e30, F32).at[0, :N_EXPERTS].set(router_b.astype(F32))
    idx, wts, rank, counts = _route(logits, bias_pad, 256)
    counts = counts[0, :N_EXPERTS].astype(I32)
    padded = (counts + R - 1) // R * R
    pend = jnp.cumsum(padded)
    pstart = pend - padded
    dest = pstart[idx[:, :TOP_K]] + rank[:, :TOP_K]
    n_blocks = -(-(T * TOP_K) // R) + N_EXPERTS
    bstart = jnp.arange(n_blocks, dtype=I32) * R
    block_e = jnp.clip(jnp.searchsorted(pend, bstart, side='right'), 0, N_EXPERTS - 1).astype(I32)
    block_valid = (bstart < pend[-1]).astype(I32)
    td = 256
    xs_zero = jnp.zeros((n_blocks * R, hp.shape[1]), U32)
    xs = _dispatch(dest.reshape(T // td, 1, td * TOP_K), hp, xs_zero, td)
    ys = _experts(block_e, block_valid, xs, wg, wu, wd)
    dest_out = dest[:n_out].reshape(n_out // tm, 1, tm * TOP_K)
    return _combine(dest_out, ys, wts, hp, x1, g2, sg.astype(BF16), su.astype(BF16), sd.astype(BF16),
                    fg, tm, tiles_per_batch, final_norm)


def _rope_tables(n):
    rows = n // GRID_W
    row = jnp.repeat(jnp.arange(rows, dtype=F32), GRID_W)
    col = jnp.tile(jnp.arange(GRID_W, dtype=F32), rows)
    half = QK_DIM // 2
    inv = 1.0 / (ROPE_THETA ** (jnp.arange(0, half, 2, dtype=F32) / half))
    ang_r = row[:, None] * inv[None, :]
    ang_c = col[:, None] * inv[None, :]
    cr, sr, cc, sn = jnp.cos(ang_r), jnp.sin(ang_r), jnp.cos(ang_c), jnp.sin(ang_c)
    cos64 = jnp.concatenate([cr, cr, cc, cc], axis=1)
    sin64 = jnp.concatenate([-sr, sr, -sn, sn], axis=1)
    reps = LANES // QK_DIM
    return jnp.tile(cos64, (1, reps)), jnp.tile(sin64, (1, reps))


def kernel(x, c, ctx, c_ctx, ada_w, ada_b, norm1_g, norm2_g, w_in, w_out, gmlp_norm_g, gmlp_ws, gmlp_bs,
           lam_q1, lam_k1, lam_q2, lam_k2, subln_g, pool_w, pool_scale, router_w, router_b, exp_gate,
           exp_up, exp_down, sh_gate, sh_up, sh_down, final_g):
    B, N, D = x.shape
    NC = ctx.shape[1]
    depth = ada_w.shape[0]
    assert depth == 2 and B + 1 <= 8

    s8 = jnp.zeros((8, D), F32).at[:B].set(jax.nn.silu(c)).at[B].set(jax.nn.silu(c_ctx))
    mod = _ada_mod(s8, ada_w, ada_b).reshape(depth, 8, 6, D)

    def mods(layer, rows):
        m = mod[layer, rows]
        sh1, s1, g1, sh2, s2, g2 = (m[:, k][:, None, :] for k in range(6))
        return sh1, 1.0 + s1, g1, sh2, 1.0 + s2, g2

    rw_pad = [jnp.zeros((D, LANES), F32).at[:, :N_EXPERTS].set(router_w[l]) for l in range(depth)]
    row = lambda v: v.reshape(1, -1).astype(F32)

    sh1, sc1, g1, sh2, sc2, g2 = mods(0, slice(0, B))
    csh1, csc1, cg1, csh2, csc2, _ = (jnp.broadcast_to(v, (B, 1, D)) for v in mods(0, slice(B, B + 1)))
    lam_init = 0.8 - 0.6 * math.exp(-0.3 * 0)
    lam = (jnp.exp(jnp.sum(lam_q1[0] * lam_k1[0])) - jnp.exp(jnp.sum(lam_q2[0] * lam_k2[0]))
           + lam_init).reshape(1).astype(F32)
    w_in_bf = w_in[0].astype(BF16)
    cos_t, sin_t = _rope_tables(N)
    qscale = QK_DIM ** -0.5
    tabs = (cos_t * qscale, sin_t * qscale, cos_t, sin_t)
    ones_c = jnp.ones((NC, LANES), F32)
    zeros_c = jnp.zeros((NC, LANES), F32)
    tabs_c = (ones_c * qscale, zeros_c, ones_c, zeros_c)

    p = _proj_in(x, row(norm1_g[0]), sh1, sc1, w_in_bf, tabs, min(512, N))
    pc = _proj_in(ctx, row(norm1_g[0]), csh1, csc1, w_in_bf, tabs_c, min(512, NC))

    gw = gmlp_norm_g.shape[1]
    gd = gw // GMLP_GROUPS
    ws_bf = gmlp_ws[0].astype(BF16)
    bs_b = jnp.broadcast_to(gmlp_bs[0][:, :, None], (GMLP_GROUPS, CHUNK, gd)).astype(F32)
    gm = _gmlp(p, row(gmlp_norm_g[0]), ws_bf, bs_b, 256)
    gmc = _gmlp(pc, row(gmlp_norm_g[0]), ws_bf, bs_b, 256)

    k0 = 2 * gw + HEADS * 2 * QK_DIM
    v0 = k0 + HEADS * 2 * QK_DIM
    k_all = jnp.concatenate([p[:, :, k0:v0], pc[:, :, k0:v0]], axis=1)

    def v_transposed(vv):
        n = vv.shape[1]
        vt = vv.reshape(B, n, HEADS, V_DIM).transpose(0, 2, 3, 1)
        ones = jnp.ones((B, HEADS, V_ROWS - V_DIM, n), BF16)
        return jnp.concatenate([vt, ones], axis=2)

    vt_c = v_transposed(pc[:, :, v0:])
    vt_all = jnp.concatenate([v_transposed(p[:, :, v0:]), vt_c], axis=3)
    g_b = jnp.broadcast_to(subln_g[0].astype(F32)[:, None], (V_DIM, LANES))
    q_col0 = 2 * gw // LANES
    out_scale = 1.0 - lam_init
    o = _attention(lam, p, q_col0, k_all, vt_all, g_b, N, 256, out_scale)
    oc = _attention(lam, pc, q_col0, pc[:, :, k0:v0], vt_c, g_b, NC, 256, out_scale)

    w_out_bf = w_out[0].astype(BF16)
    x1, hp, lg = _proj_out(gm, o, w_out_bf, x, g1, row(norm2_g[0]), sh2, sc2, rw_pad[0], 256)
    _, hpc, lgc = _proj_out(gmc, oc, w_out_bf, ctx, cg1, row(norm2_g[0]), csh2, csc2, rw_pad[0], 256)

    T_lat = B * N
    hp_all = jnp.concatenate([hp.reshape(T_lat, -1), hpc.reshape(B * NC, -1)], axis=0)
    lg_all = jnp.concatenate([lg.reshape(T_lat, LANES), lgc.reshape(B * NC, LANES)], axis=0)
    tm_c = 128
    x = _moe(hp_all, lg_all, x1.reshape(T_lat, D), g2, T_lat, N // tm_c, tm_c, router_b[0],
             exp_gate[0], exp_up[0], exp_down[0], sh_gate[0], sh_up[0], sh_down[0],
             row(final_g), False).reshape(B, N, D)

    sh1, sc1, g1, sh2, sc2, g2 = mods(1, slice(0, B))
    x1, hp, lg = _pool_mixer(x, row(norm1_g[1]), sh1, sc1, pool_w[0].astype(BF16), row(pool_scale[0]),
                             g1, row(norm2_g[1]), sh2, sc2, rw_pad[1], 256)
    out = _moe(hp.reshape(T_lat, -1), lg.reshape(T_lat, LANES), x1.reshape(T_lat, D), g2, T_lat,
               N // tm_c, tm_c, router_b[1], exp_gate[1], exp_up[1], exp_down[1],
               sh_gate[1], sh_up[1], sh_down[1], row(final_g), True)
    return out.reshape(B, N, D)
```

```python
import functools
import math

import jax
import jax.numpy as jnp
from jax import lax
from jax.experimental import pallas as pl
from jax.experimental.pallas import tpu as pltpu

EPS = 1e-6
GRID_W = 64
ROPE_THETA = 10000.0
HEADS = 8
QK_DIM = 64
V_DIM = 128
V_ROWS = V_DIM + 8
GMLP_GROUPS = 4
CHUNK = 128
POOL_WINDOWS = (2, 4, 8, 16)
POOL_HALO = 8
N_EXPERTS = 64
TOP_K = 8
ROUTED_SCALE = 2.5
LANES = 128
EXPERT_ROWS = 256
SAFE_SCORE_BOUND = 30.0
VMEM_LIMIT = 56 * 1024 * 1024

BF16 = jnp.bfloat16
F32 = jnp.float32
U32 = jnp.uint32
I32 = jnp.int32


def _cparams(sem):
    return pltpu.CompilerParams(dimension_semantics=sem, vmem_limit_bytes=VMEM_LIMIT)


def _dot(a, b):
    return jnp.dot(a, b, preferred_element_type=F32)


def _pack_pair(lo, hi):
    ulo = pltpu.bitcast(lo.astype(BF16).astype(F32), U32)
    uhi = pltpu.bitcast(hi.astype(BF16).astype(F32), U32)
    return (ulo >> 16) | uhi


def _unpack_pair(u):
    lo = pltpu.bitcast(u << 16, F32)
    hi = pltpu.bitcast(u & jnp.uint32(0xFFFF0000), F32)
    return lo, hi


def _rms_mod(x, g, shift, scale1p):
    y = x * lax.rsqrt(jnp.mean(x * x, axis=-1, keepdims=True) + EPS)
    return y * g * scale1p + shift


def _sigmoid(x):
    return 1.0 / (1.0 + jnp.exp(-x))


def _ada_kernel(s_ref, w_ref, b_ref, o_ref):
    o_ref[0] = _dot(s_ref[...].astype(BF16), w_ref[0].astype(BF16)) + b_ref[0]


def _ada_mod(s8, ada_w, ada_b):
    L, D, W = ada_w.shape
    tn = 768
    return pl.pallas_call(
        _ada_kernel,
        out_shape=jax.ShapeDtypeStruct((L, 8, W), F32),
        grid=(L, W // tn),
        in_specs=[pl.BlockSpec((8, D), lambda l, j: (0, 0)),
                  pl.BlockSpec((1, D, tn), lambda l, j: (l, 0, j)),
                  pl.BlockSpec((1, 1, tn), lambda l, j: (l, 0, j))],
        out_specs=pl.BlockSpec((1, 8, tn), lambda l, j: (l, 0, j)),
        compiler_params=_cparams(("arbitrary", "arbitrary")),
        name="ada_mod",
    )(s8, ada_w, ada_b.reshape(L, 1, W))


def _proj_in_kernel(x_ref, g_ref, sh_ref, sc_ref, w_ref, cq_ref, sq_ref, ck_ref, sk_ref,
                    o_ref, h_ref, *, q_tile, k_tile):
    j = pl.program_id(2)

    @pl.when(j == 0)
    def _():
        h_ref[...] = _rms_mod(x_ref[0], g_ref[...], sh_ref[0], sc_ref[0]).astype(BF16)

    acc = _dot(h_ref[...], w_ref[...])
    tn = acc.shape[1]

    def rope(cos_ref, sin_ref):
        reps = tn // LANES
        cos = jnp.concatenate([cos_ref[...]] * reps, axis=1)
        sin = jnp.concatenate([sin_ref[...]] * reps, axis=1)
        lane = lax.broadcasted_iota(I32, acc.shape, 1)
        partner = jnp.where((lane & 16) == 0,
                            pltpu.roll(acc, tn - 16, axis=1),
                            pltpu.roll(acc, 16, axis=1))
        return acc * cos + partner * sin

    is_q = j == q_tile
    is_k = j == k_tile

    @pl.when(is_q)
    def _():
        o_ref[0] = rope(cq_ref, sq_ref).astype(o_ref.dtype)

    @pl.when(is_k)
    def _():
        o_ref[0] = rope(ck_ref, sk_ref).astype(o_ref.dtype)

    @pl.when(jnp.logical_not(jnp.logical_or(is_q, is_k)))
    def _():
        o_ref[0] = acc.astype(o_ref.dtype)


def _proj_in(x, g, shift, scale1p, w_bf, tabs, tm):
    B, N, D = x.shape
    W = w_bf.shape[1]
    tn = 1024
    q_tile, k_tile = 2, 3
    kern = functools.partial(_proj_in_kernel, q_tile=q_tile, k_tile=k_tile)
    vec = pl.BlockSpec((1, 1, D), lambda b, i, j: (b, 0, 0))
    tab = pl.BlockSpec((tm, LANES), lambda b, i, j: (i, 0))
    return pl.pallas_call(
        kern,
        out_shape=jax.ShapeDtypeStruct((B, N, W), BF16),
        grid=(B, N // tm, W // tn),
        in_specs=[pl.BlockSpec((1, tm, D), lambda b, i, j: (b, i, 0)),
                  pl.BlockSpec((1, D), lambda b, i, j: (0, 0)),
                  vec, vec,
                  pl.BlockSpec((D, tn), lambda b, i, j: (0, j)),
                  tab, tab, tab, tab],
        out_specs=pl.BlockSpec((1, tm, tn), lambda b, i, j: (b, i, j)),
        scratch_shapes=[pltpu.VMEM((tm, D), BF16)],
        compiler_params=_cparams(("arbitrary", "arbitrary", "arbitrary")),
        name="proj_in",
    )(x, g, shift, scale1p, w_bf, *tabs)


def _gelu_tanh(x):
    return 0.5 * x * (1.0 + jnp.tanh(math.sqrt(2.0 / math.pi) * (x + 0.044715 * (x * x * x))))


def _gmlp_kernel(uv_ref, ng_ref, ws_ref, bs_ref, o_ref, *, chunks):
    gw = o_ref.shape[2]
    gd = gw // GMLP_GROUPS
    for c in range(chunks):
        rows = pl.ds(c * CHUNK, CHUNK)
        for g in range(GMLP_GROUPS):
            u = _gelu_tanh(uv_ref[0, rows, g * gd:(g + 1) * gd].astype(F32))
            v = _gelu_tanh(uv_ref[0, rows, gw + g * gd:gw + (g + 1) * gd].astype(F32))
            vn = v * lax.rsqrt(jnp.mean(v * v, axis=-1, keepdims=True) + EPS) * ng_ref[:, g * gd:(g + 1) * gd]
            mixed = _dot(ws_ref[g], vn.astype(BF16)) + bs_ref[g]
            o_ref[0, rows, g * gd:(g + 1) * gd] = (u * mixed).astype(o_ref.dtype)


def _gmlp(p, norm_g, ws_bf, bs_b, tg):
    B, N, _ = p.shape
    gw = norm_g.shape[1]
    gd = gw // GMLP_GROUPS
    kern = functools.partial(_gmlp_kernel, chunks=tg // CHUNK)
    return pl.pallas_call(
        kern,
        out_shape=jax.ShapeDtypeStruct((B, N, gw), BF16),
        grid=(B, N // tg),
        in_specs=[pl.BlockSpec((1, tg, 2 * gw), lambda b, i: (b, i, 0)),
                  pl.BlockSpec((1, gw), lambda b, i: (0, 0)),
                  pl.BlockSpec((GMLP_GROUPS, CHUNK, CHUNK), lambda b, i: (0, 0, 0)),
                  pl.BlockSpec((GMLP_GROUPS, CHUNK, gd), lambda b, i: (0, 0, 0))],
        out_specs=pl.BlockSpec((1, tg, gw), lambda b, i: (b, i, 0)),
        compiler_params=_cparams(("arbitrary", "arbitrary")),
        name="gmlp",
    )(p, norm_g, ws_bf, bs_b)


def _attn_kernel(lam_ref, q_ref, k_ref, vt_ref, g_ref, o_ref, kmax_ref, acc1_ref, acc2_ref,
                 sa1_ref, sa2_ref, sb1_ref, sb2_ref, *, tk, out_scale):
    i = pl.program_id(2)
    tq = q_ref.shape[1]
    nk = k_ref.shape[1]
    nt = nk // tk
    ts = 256
    nt_dims = (((1,), (1,)), ((), ()))
    sub = lax.broadcasted_iota(I32, (8, LANES), 0)
    ln8 = lax.broadcasted_iota(I32, (8, LANES), 1)
    half_sel = jnp.where(jnp.logical_or(jnp.logical_and(sub == 0, ln8 < QK_DIM),
                                        jnp.logical_and(sub == 1, ln8 >= QK_DIM)), 1.0, 0.0).astype(BF16)

    def sq_norms(rows):
        rf = rows.astype(F32)
        return lax.dot_general(half_sel, (rf * rf).astype(BF16), nt_dims, preferred_element_type=F32)

    def k_tile(t):
        start = pl.multiple_of(t * tk, tk)
        return k_ref[0, pl.ds(start, tk), :], vt_ref[0, 0, :, pl.ds(start, tk)]

    @pl.when(i == 0)
    def _():
        def body(t, mx):
            return jnp.maximum(mx, jnp.max(sq_norms(k_tile(t)[0]), axis=1, keepdims=True))
        mx = lax.fori_loop(0, nt, body, jnp.zeros((8, 1), F32))
        kmax_ref[...] = jnp.broadcast_to(mx, (8, LANES))

    q = q_ref[0]
    lane = lax.broadcasted_iota(I32, q.shape, 1)
    zero = jnp.zeros_like(q)
    qa = jnp.where(lane < QK_DIM, q, zero)
    qb = jnp.where(lane >= QK_DIM, q, zero)
    bound = 1.01 * jnp.sqrt(sq_norms(q) * kmax_ref[:, 0:1]) + 1e-3
    b1, b2 = bound[0:1], bound[1:2]
    safe = jnp.max(bound[0:2]) <= SAFE_SCORE_BOUND
    a0 = jnp.zeros((V_ROWS, tq), F32)

    @pl.when(safe)
    def _():
        bufs = ((sa1_ref, sa2_ref), (sb1_ref, sb2_ref))

        def qk(t, buf):
            k, _ = k_tile(t)
            s1 = lax.dot_general(k, qa, nt_dims, preferred_element_type=F32)
            buf[0][...] = jnp.exp2(s1 - b1).astype(BF16)
            s2 = lax.dot_general(k, qb, nt_dims, preferred_element_type=F32)
            buf[1][...] = jnp.exp2(s2 - b2).astype(BF16)

        def pv(t, buf, a1, a2):
            _, vt = k_tile(t)
            a1 = a1 + _dot(vt, buf[0][...])
            a2 = a2 + _dot(vt, buf[1][...])
            return a1, a2

        qk(0, bufs[0])

        def pair(j, carry):
            a1, a2 = carry
            t = 2 * j
            qk(t + 1, bufs[1])
            a1, a2 = pv(t, bufs[0], a1, a2)
            qk(t + 2, bufs[0])
            return pv(t + 1, bufs[1], a1, a2)

        a1, a2 = lax.fori_loop(0, (nt - 1) // 2, pair, (a0, a0))
        if (nt - 1) % 2 == 1:
            qk(nt - 1, bufs[1])
            a1, a2 = pv(nt - 2, bufs[0], a1, a2)
            a1, a2 = pv(nt - 1, bufs[1], a1, a2)
        else:
            a1, a2 = pv(nt - 1, bufs[0], a1, a2)
        acc1_ref[...] = a1
        acc2_ref[...] = a2

    @pl.when(jnp.logical_not(safe))
    def _():
        def one_map(qm, k, vt, m, acc):
            s = lax.dot_general(k, qm, nt_dims, preferred_element_type=F32)
            m_new = jnp.maximum(m, jnp.max(s, axis=0, keepdims=True))
            alpha = jnp.exp2(m - m_new)
            p = jnp.exp2(s - m_new).astype(BF16)
            return m_new, alpha * acc + _dot(vt, p)

        def step(t, carry):
            m1, a1, m2, a2 = carry
            k, vt = k_tile(t)
            m1, a1 = one_map(qa, k, vt, m1, a1)
            m2, a2 = one_map(qb, k, vt, m2, a2)
            return m1, a1, m2, a2

        m0 = jnp.full((1, tq), -jnp.inf, F32)
        _, a1, _, a2 = lax.fori_loop(0, nt, step, (m0, a0, m0, a0))
        acc1_ref[...] = a1
        acc2_ref[...] = a2

    a1 = acc1_ref[...]
    a2 = acc2_ref[...]
    lam = lam_ref[0]
    o = a1[:V_DIM] / a1[V_DIM:V_DIM + 1] - lam * (a2[:V_DIM] / a2[V_DIM:V_DIM + 1])
    o = o * lax.rsqrt(jnp.mean(o * o, axis=0, keepdims=True) + EPS)
    reps = tq // LANES
    gain = jnp.concatenate([g_ref[...]] * reps, axis=1) if reps > 1 else g_ref[...]
    o_ref[0] = (o * gain * out_scale).T.astype(o_ref.dtype)


def _attention(lam, q_src, q_col0, k_all, vt, g_b, nq, tq, out_scale):
    B, nk, _ = k_all.shape
    tk = 1280 if nk % 1280 == 0 else 256
    assert nk % tk == 0 and nq % tq == 0
    kern = functools.partial(_attn_kernel, tk=tk, out_scale=out_scale)
    return pl.pallas_call(
        kern,
        out_shape=jax.ShapeDtypeStruct((B, nq, HEADS * V_DIM), BF16),
        grid=(B, HEADS, nq // tq),
        in_specs=[pl.BlockSpec(memory_space=pltpu.SMEM),
                  pl.BlockSpec((1, tq, LANES), lambda b, h, i: (b, i, q_col0 + h)),
                  pl.BlockSpec((1, nk, LANES), lambda b, h, i: (b, 0, h)),
                  pl.BlockSpec((1, 1, V_ROWS, nk), lambda b, h, i: (b, h, 0, 0)),
                  pl.BlockSpec((V_DIM, LANES), lambda b, h, i: (0, 0))],
        out_specs=pl.BlockSpec((1, tq, LANES), lambda b, h, i: (b, i, h)),
        scratch_shapes=[pltpu.VMEM((8, LANES), F32),
                        pltpu.VMEM((V_ROWS, tq), F32),
                        pltpu.VMEM((V_ROWS, tq), F32)] + [pltpu.VMEM((tk, tq), BF16)] * 4,
        compiler_params=_cparams(("arbitrary", "arbitrary", "arbitrary")),
        name="diff_attn",
    )(lam, q_src, k_all, vt, g_b)


def _mixer_epilogue(x, y, g1, n2g, sh2, sc2, rw, x1_ref, hp_ref, lg_ref):
    x1 = x + g1 * y
    x1_ref[0] = x1
    h2 = _rms_mod(x1, n2g, sh2, sc2)
    half = h2.shape[1] // 2
    hp_ref[0] = _pack_pair(h2[:, :half], h2[:, half:])
    lg_ref[0] = _dot(h2, rw)


def _proj_out_kernel(gm_ref, o_ref, w_ref, x_ref, g1_ref, n2g_ref, sh2_ref, sc2_ref, rw_ref,
                     x1_ref, hp_ref, lg_ref):
    gw = gm_ref.shape[2]
    y = _dot(gm_ref[0], w_ref[:gw, :]) + _dot(o_ref[0], w_ref[gw:, :])
    _mixer_epilogue(x_ref[0], y, g1_ref[0], n2g_ref[...], sh2_ref[0], sc2_ref[0], rw_ref[...],
                    x1_ref, hp_ref, lg_ref)


def _epilogue_out(B, N, D, tm):
    shapes = (jax.ShapeDtypeStruct((B, N, D), F32),
              jax.ShapeDtypeStruct((B, N, D // 2), U32),
              jax.ShapeDtypeStruct((B, N, LANES), F32))
    specs = (pl.BlockSpec((1, tm, D), lambda b, i: (b, i, 0)),
             pl.BlockSpec((1, tm, D // 2), lambda b, i: (b, i, 0)),
             pl.BlockSpec((1, tm, LANES), lambda b, i: (b, i, 0)))
    return shapes, specs


def _proj_out(gm, o, w_bf, x, g1, n2g, sh2, sc2, rw_pad, tm):
    B, N, D = x.shape
    gw = gm.shape[2]
    vec = pl.BlockSpec((1, 1, D), lambda b, i: (b, 0, 0))
    shapes, specs = _epilogue_out(B, N, D, tm)
    return pl.pallas_call(
        _proj_out_kernel,
        out_shape=shapes,
        grid=(B, N // tm),
        in_specs=[pl.BlockSpec((1, tm, gw), lambda b, i: (b, i, 0)),
                  pl.BlockSpec((1, tm, o.shape[2]), lambda b, i: (b, i, 0)),
                  pl.BlockSpec(w_bf.shape, lambda b, i: (0, 0)),
                  pl.BlockSpec((1, tm, D), lambda b, i: (b, i, 0)),
                  vec,
                  pl.BlockSpec((1, D), lambda b, i: (0, 0)),
                  vec, vec,
                  pl.BlockSpec((D, LANES), lambda b, i: (0, 0))],
        out_specs=specs,
        compiler_params=_cparams(("arbitrary", "arbitrary")),
        name="proj_out",
    )(gm, o, w_bf, x, g1, n2g, sh2, sc2, rw_pad)


def _pool_kernel(x_ref, xp_ref, xn_ref, n1g_ref, sh1_ref, sc1_ref, pw_ref, ps_ref,
                 g1_ref, n2g_ref, sh2_ref, sc2_ref, rw_ref, x1_ref, hp_ref, lg_ref, y_ref, *, seq):
    i = pl.program_id(1)
    last = pl.num_programs(1) - 1
    x = x_ref[0]
    tm, D = x.shape
    n1g, sh1, sc1 = n1g_ref[...], sh1_ref[0], sc1_ref[0]
    hc = _rms_mod(x, n1g, sh1, sc1)
    hp = _rms_mod(xp_ref[0], n1g, sh1, sc1) * (i > 0).astype(F32)
    hn = _rms_mod(xn_ref[0], n1g, sh1, sc1) * (i < last).astype(F32)
    he = jnp.concatenate([hp, hc, hn], axis=0).astype(BF16)
    groups = len(POOL_WINDOWS)
    gd = D // groups
    row = lax.broadcasted_iota(I32, (tm, tm + 2 * POOL_HALO), 0)
    col = lax.broadcasted_iota(I32, (tm, tm + 2 * POOL_HALO), 1)
    tglob = i * tm + lax.broadcasted_iota(I32, (tm, 1), 0)
    for g, w in enumerate(POOL_WINDOWS):
        lo = row + (POOL_HALO - w // 2)
        band = jnp.where(jnp.logical_and(col >= lo, col < lo + w), 1.0, 0.0).astype(BF16)
        s = _dot(band, he[:, g * gd:(g + 1) * gd])
        cnt = jnp.minimum(tglob + w // 2, seq) - jnp.maximum(tglob - w // 2, 0)
        pooled = s / cnt.astype(F32) - hc[:, g * gd:(g + 1) * gd]
        y_ref[:, g * gd:(g + 1) * gd] = _dot(pooled.astype(BF16), pw_ref[g])
    y = y_ref[...] * ps_ref[...]
    _mixer_epilogue(x, y, g1_ref[0], n2g_ref[...], sh2_ref[0], sc2_ref[0], rw_ref[...],
                    x1_ref, hp_ref, lg_ref)


def _pool_mixer(x, n1g, sh1, sc1, pw_bf, ps, g1, n2g, sh2, sc2, rw_pad, tm):
    B, N, D = x.shape
    gd = D // len(POOL_WINDOWS)
    hb = tm // POOL_HALO
    nhb = N // POOL_HALO
    vec = pl.BlockSpec((1, 1, D), lambda b, i: (b, 0, 0))
    row = pl.BlockSpec((1, D), lambda b, i: (0, 0))
    shapes, specs = _epilogue_out(B, N, D, tm)
    kern = functools.partial(_pool_kernel, seq=N)
    return pl.pallas_call(
        kern,
        out_shape=shapes,
        grid=(B, N // tm),
        in_specs=[pl.BlockSpec((1, tm, D), lambda b, i: (b, i, 0)),
                  pl.BlockSpec((1, POOL_HALO, D), lambda b, i: (b, jnp.maximum(i * hb - 1, 0), 0)),
                  pl.BlockSpec((1, POOL_HALO, D), lambda b, i: (b, jnp.minimum((i + 1) * hb, nhb - 1), 0)),
                  row, vec, vec,
                  pl.BlockSpec((len(POOL_WINDOWS), gd, gd), lambda b, i: (0, 0, 0)),
                  row, vec, row, vec, vec,
                  pl.BlockSpec((D, LANES), lambda b, i: (0, 0))],
        out_specs=specs,
        scratch_shapes=[pltpu.VMEM((tm, D), F32)],
        compiler_params=_cparams(("arbitrary", "arbitrary")),
        name="pool_mixer",
    )(x, x, x, n1g, sh1, sc1, pw_bf, ps, g1, n2g, sh2, sc2, rw_pad)


def _route_kernel(lg_ref, b_ref, idx_ref, w_ref, rank_ref, cnt_ref, carry_ref):
    @pl.when(pl.program_id(0) == 0)
    def _():
        carry_ref[...] = jnp.zeros_like(carry_ref)

    logits = lg_ref[...]
    tr = logits.shape[0]
    scores = _sigmoid(logits)
    sel = scores + b_ref[...]
    lane = lax.broadcasted_iota(I32, (tr, LANES), 1)
    picked = jnp.zeros((tr, LANES), F32)
    onehots, wk = [], []
    for _ in range(TOP_K):
        mx = jnp.max(sel, axis=1, keepdims=True)
        first = jnp.min(jnp.where(sel == mx, lane, LANES), axis=1, keepdims=True)
        oh = lane == first
        onehots.append(oh)
        wk.append(jnp.sum(jnp.where(oh, scores, 0.0), axis=1, keepdims=True))
        sel = jnp.where(oh, -jnp.inf, sel)
        picked = jnp.where(oh, 1.0, picked)
    wsum = wk[0]
    for v in wk[1:]:
        wsum = wsum + v
    r = lax.broadcasted_iota(I32, (tr, tr), 0)
    c = lax.broadcasted_iota(I32, (tr, tr), 1)
    below = jnp.where(c < r, 1.0, 0.0).astype(BF16)
    before = _dot(below, picked.astype(BF16)) + carry_ref[...]
    idx_o = jnp.zeros((tr, LANES), I32)
    w_o = jnp.zeros((tr, LANES), F32)
    rank_o = jnp.zeros((tr, LANES), I32)
    for k in range(TOP_K):
        oh = onehots[k]
        first = jnp.sum(jnp.where(oh, lane, 0), axis=1, keepdims=True)
        rk = jnp.sum(jnp.where(oh, before, 0.0), axis=1, keepdims=True).astype(I32)
        idx_o = jnp.where(lane == k, first, idx_o)
        w_o = jnp.where(lane == k, wk[k] / wsum * ROUTED_SCALE, w_o)
        rank_o = jnp.where(lane == k, rk, rank_o)
    idx_ref[...] = idx_o
    w_ref[...] = w_o
    rank_ref[...] = rank_o
    carry_ref[...] = carry_ref[...] + jnp.sum(picked, axis=0, keepdims=True)
    cnt_ref[...] = carry_ref[...]


def _route(logits, bias_pad, tr):
    T = logits.shape[0]
    blk = pl.BlockSpec((tr, LANES), lambda i: (i, 0))
    one = pl.BlockSpec((1, LANES), lambda i: (0, 0))
    return pl.pallas_call(
        _route_kernel,
        out_shape=(jax.ShapeDtypeStruct((T, LANES), I32),
                   jax.ShapeDtypeStruct((T, LANES), F32),
                   jax.ShapeDtypeStruct((T, LANES), I32),
                   jax.ShapeDtypeStruct((1, LANES), F32)),
        grid=(T // tr,),
        in_specs=[blk, one],
        out_specs=(blk, blk, blk, one),
        scratch_shapes=[pltpu.VMEM((1, LANES), F32)],
        compiler_params=_cparams(("arbitrary",)),
        name="route",
    )(logits, bias_pad)


def _dispatch_kernel(dest_ref, h_ref, xs_in_ref, xs_ref, sem):
    del xs_in_ref
    td = h_ref.shape[0]

    def row_copy(r, k):
        d = dest_ref[0, 0, r * TOP_K + k]
        return pltpu.make_async_copy(h_ref.at[pl.ds(r, 1), :], xs_ref.at[pl.ds(d, 1), :], sem)

    def issue(r, _):
        for k in range(TOP_K):
            row_copy(r, k).start()
        return 0

    lax.fori_loop(0, td, issue, 0)

    def drain(r, _):
        for k in range(TOP_K):
            row_copy(r, k).wait()
        return 0

    lax.fori_loop(0, td, drain, 0)


def _dispatch(dest3, hp, xs_zero, td):
    T, W = hp.shape
    return pl.pallas_call(
        _dispatch_kernel,
        out_shape=jax.ShapeDtypeStruct(xs_zero.shape, U32),
        grid=(T // td,),
        in_specs=[pl.BlockSpec((1, 1, td * TOP_K), lambda i: (i, 0, 0), memory_space=pltpu.SMEM),
                  pl.BlockSpec((td, W), lambda i: (i, 0)),
                  pl.BlockSpec(memory_space=pl.ANY)],
        out_specs=pl.BlockSpec(memory_space=pl.ANY),
        scratch_shapes=[pltpu.SemaphoreType.DMA(())],
        input_output_aliases={2: 0},
        compiler_params=_cparams(("arbitrary",)),
        name="dispatch",
    )(dest3, hp, xs_zero)


def _experts_kernel(be_ref, bv_ref, xs_ref, wg_ref, wu_ref, wd_ref, ys_ref, wgb, wub, wdb):
    i = pl.program_id(0)
    valid = bv_ref[i] != 0
    fresh = jnp.logical_or(i == 0, be_ref[i] != be_ref[jnp.maximum(i - 1, 0)])

    @pl.when(jnp.logical_and(valid, fresh))
    def _():
        wgb[...] = wg_ref[0].astype(BF16)
        wub[...] = wu_ref[0].astype(BF16)
        wdb[...] = wd_ref[0].astype(BF16)

    @pl.when(valid)
    def _():
        lo, hi = _unpack_pair(xs_ref[...])
        lo, hi = lo.astype(BF16), hi.astype(BF16)
        half = lo.shape[1]
        g = _dot(lo, wgb[:half, :]) + _dot(hi, wgb[half:, :])
        u = _dot(lo, wub[:half, :]) + _dot(hi, wub[half:, :])
        a = (g * _sigmoid(g) * u).astype(BF16)
        y = _dot(a, wdb[...])
        ys_ref[...] = _pack_pair(y[:, :half], y[:, half:])

    @pl.when(jnp.logical_not(valid))
    def _():
        ys_ref[...] = jnp.zeros_like(ys_ref)


def _experts(block_e, block_valid, xs, wg, wu, wd):
    P, W = xs.shape
    _, D, ED = wg.shape
    R = EXPERT_ROWS
    grid_spec = pltpu.PrefetchScalarGridSpec(
        num_scalar_prefetch=2,
        grid=(P // R,),
        in_specs=[pl.BlockSpec((R, W), lambda i, be, bv: (i, 0)),
                  pl.BlockSpec((1, D, ED), lambda i, be, bv: (be[i], 0, 0)),
                  pl.BlockSpec((1, D, ED), lambda i, be, bv: (be[i], 0, 0)),
                  pl.BlockSpec((1, ED, D), lambda i, be, bv: (be[i], 0, 0))],
        out_specs=pl.BlockSpec((R, W), lambda i, be, bv: (i, 0)),
        scratch_shapes=[pltpu.VMEM((D, ED), BF16), pltpu.VMEM((D, ED), BF16), pltpu.VMEM((ED, D), BF16)],
    )
    return pl.pallas_call(
        _experts_kernel,
        out_shape=jax.ShapeDtypeStruct((P, W), U32),
        grid_spec=grid_spec,
        compiler_params=_cparams(("arbitrary",)),
        name="experts",
    )(block_e, block_valid, xs, wg, wu, wd)


def _combine_kernel(dest_ref, ys_ref, wts_ref, hp_ref, x_ref, g2_ref, sg_ref, su_ref, sd_ref, fg_ref,
                    o_ref, buf, sem, *, final_norm):
    tm = hp_ref.shape[0]

    def row_copy(r, k):
        d = dest_ref[0, 0, r * TOP_K + k]
        return pltpu.make_async_copy(ys_ref.at[pl.ds(d, 1), :], buf.at[k, pl.ds(r, 1), :], sem)

    def issue(r, _):
        for k in range(TOP_K):
            row_copy(r, k).start()
        return 0

    lax.fori_loop(0, tm, issue, 0)

    lo, hi = _unpack_pair(hp_ref[...])
    lo, hi = lo.astype(BF16), hi.astype(BF16)
    half = lo.shape[1]
    g = _dot(lo, sg_ref[:half, :]) + _dot(hi, sg_ref[half:, :])
    u = _dot(lo, su_ref[:half, :]) + _dot(hi, su_ref[half:, :])
    shared = _dot((g * _sigmoid(g) * u).astype(BF16), sd_ref[...])

    def drain(r, _):
        for k in range(TOP_K):
            row_copy(r, k).wait()
        return 0

    lax.fori_loop(0, tm, drain, 0)

    wts = wts_ref[...]
    acc_lo = shared[:, :half]
    acc_hi = shared[:, half:]
    for k in range(TOP_K):
        ylo, yhi = _unpack_pair(buf[k])
        wk = wts[:, k:k + 1]
        acc_lo = acc_lo + wk * ylo
        acc_hi = acc_hi + wk * yhi
    g2 = g2_ref[0]
    x = x_ref[...]
    out_lo = x[:, :half] + g2[:, :half] * acc_lo
    out_hi = x[:, half:] + g2[:, half:] * acc_hi
    if final_norm:
        ms = (jnp.sum(out_lo * out_lo, axis=-1, keepdims=True)
              + jnp.sum(out_hi * out_hi, axis=-1, keepdims=True)) / (2 * half)
        inv = lax.rsqrt(ms + EPS)
        fg = fg_ref[...]
        out_lo = out_lo * inv * fg[:, :half]
        out_hi = out_hi * inv * fg[:, half:]
    o_ref[:, :half] = out_lo
    o_ref[:, half:] = out_hi


def _combine(dest3, ys, wts, hp, x1, g2, sg_bf, su_bf, sd_bf, fg, tm, tiles_per_batch, final_norm):
    T, D = x1.shape
    W = D // 2
    ED = sg_bf.shape[1]
    kern = functools.partial(_combine_kernel, final_norm=final_norm)
    return pl.pallas_call(
        kern,
        out_shape=jax.ShapeDtypeStruct((T, D), F32),
        grid=(T // tm,),
        in_specs=[pl.BlockSpec((1, 1, tm * TOP_K), lambda i: (i, 0, 0), memory_space=pltpu.SMEM),
                  pl.BlockSpec(memory_space=pl.ANY),
                  pl.BlockSpec((tm, LANES), lambda i: (i, 0)),
                  pl.BlockSpec((tm, W), lambda i: (i, 0)),
                  pl.BlockSpec((tm, D), lambda i: (i, 0)),
                  pl.BlockSpec((1, 1, D), lambda i: (i // tiles_per_batch, 0, 0)),
                  pl.BlockSpec((D, ED), lambda i: (0, 0)),
                  pl.BlockSpec((D, ED), lambda i: (0, 0)),
                  pl.BlockSpec((ED, D), lambda i: (0, 0)),
                  pl.BlockSpec((1, D), lambda i: (0, 0))],
        out_specs=pl.BlockSpec((tm, D), lambda i: (i, 0)),
        scratch_shapes=[pltpu.VMEM((TOP_K, tm, W), U32), pltpu.SemaphoreType.DMA(())],
        compiler_params=_cparams(("arbitrary",)),
        name="combine",
    )(dest3, ys, wts, hp, x1, g2, sg_bf, su_bf, sd_bf, fg)


def _moe(hp, logits, x1, g2, n_out, tiles_per_batch, tm, router_b, wg, wu, wd, sg, su, sd, fg, final_norm):
    T = hp.shape[0]
    R = EXPERT_ROWS
    bias_pad = jnp.full((1, LANES), -1e30, F32).at[0, :N_EXPERTS].set(router_b.astype(F32))
    idx, wts, rank, counts = _route(logits, bias_pad, 256)
    counts = counts[0, :N_EXPERTS].astype(I32)
    padded = (counts + R - 1) // R * R
    pend = jnp.cumsum(padded)
    pstart = pend - padded
    dest = pstart[idx[:, :TOP_K]] + rank[:, :TOP_K]
    n_blocks = -(-(T * TOP_K) // R) + N_EXPERTS
    bstart = jnp.arange(n_blocks, dtype=I32) * R
    block_e = jnp.minimum(jnp.sum((pend[None, :] <= bstart[:, None]).astype(I32), axis=1), N_EXPERTS - 1)
    block_valid = (bstart < pend[-1]).astype(I32)
    td = 256
    xs_zero = jnp.zeros((n_blocks * R, hp.shape[1]), U32)
    xs = _dispatch(dest.reshape(T // td, 1, td * TOP_K), hp, xs_zero, td)
    ys = _experts(block_e, block_valid, xs, wg, wu, wd)
    dest_out = dest[:n_out].reshape(n_out // tm, 1, tm * TOP_K)
    return _combine(dest_out, ys, wts, hp, x1, g2, sg.astype(BF16), su.astype(BF16), sd.astype(BF16),
                    fg, tm, tiles_per_batch, final_norm)


def _rope_tables(n):
    rows = n // GRID_W
    row = jnp.repeat(jnp.arange(rows, dtype=F32), GRID_W)
    col = jnp.tile(jnp.arange(GRID_W, dtype=F32), rows)
    half = QK_DIM // 2
    inv = 1.0 / (ROPE_THETA ** (jnp.arange(0, half, 2, dtype=F32) / half))
    ang_r = row[:, None] * inv[None, :]
    ang_c = col[:, None] * inv[None, :]
    cr, sr, cc, sn = jnp.cos(ang_r), jnp.sin(ang_r), jnp.cos(ang_c), jnp.sin(ang_c)
    cos64 = jnp.concatenate([cr, cr, cc, cc], axis=1)
    sin64 = jnp.concatenate([-sr, sr, -sn, sn], axis=1)
    reps = LANES // QK_DIM
    return jnp.tile(cos64, (1, reps)), jnp.tile(sin64, (1, reps))


def kernel(x, c, ctx, c_ctx, ada_w, ada_b, norm1_g, norm2_g, w_in, w_out, gmlp_norm_g, gmlp_ws, gmlp_bs,
           lam_q1, lam_k1, lam_q2, lam_k2, subln_g, pool_w, pool_scale, router_w, router_b, exp_gate,
           exp_up, exp_down, sh_gate, sh_up, sh_down, final_g):
    B, N, D = x.shape
    NC = ctx.shape[1]
    depth = ada_w.shape[0]
    assert depth == 2 and B + 1 <= 8

    s8 = jnp.zeros((8, D), F32).at[:B].set(jax.nn.silu(c)).at[B].set(jax.nn.silu(c_ctx))
    mod = _ada_mod(s8, ada_w, ada_b).reshape(depth, 8, 6, D)

    def mods(layer, rows):
        m = mod[layer, rows]
        sh1, s1, g1, sh2, s2, g2 = (m[:, k][:, None, :] for k in range(6))
        return sh1, 1.0 + s1, g1, sh2, 1.0 + s2, g2

    rw_pad = [jnp.zeros((D, LANES), F32).at[:, :N_EXPERTS].set(router_w[l]) for l in range(depth)]
    row = lambda v: v.reshape(1, -1).astype(F32)

    sh1, sc1, g1, sh2, sc2, g2 = mods(0, slice(0, B))
    csh1, csc1, cg1, csh2, csc2, _ = (jnp.broadcast_to(v, (B, 1, D)) for v in mods(0, slice(B, B + 1)))
    lam_init = 0.8 - 0.6 * math.exp(-0.3 * 0)
    lam = (jnp.exp(jnp.sum(lam_q1[0] * lam_k1[0])) - jnp.exp(jnp.sum(lam_q2[0] * lam_k2[0]))
           + lam_init).reshape(1).astype(F32)
    w_in_bf = w_in[0].astype(BF16)
    cos_t, sin_t = _rope_tables(N)
    qscale = QK_DIM ** -0.5 * math.log2(math.e)
    tabs = (cos_t * qscale, sin_t * qscale, cos_t, sin_t)
    ones_c = jnp.ones((NC, LANES), F32)
    zeros_c = jnp.zeros((NC, LANES), F32)
    tabs_c = (ones_c * qscale, zeros_c, ones_c, zeros_c)

    p = _proj_in(x, row(norm1_g[0]), sh1, sc1, w_in_bf, tabs, min(512, N))
    pc = _proj_in(ctx, row(norm1_g[0]), csh1, csc1, w_in_bf, tabs_c, min(512, NC))

    gw = gmlp_norm_g.shape[1]
    gd = gw // GMLP_GROUPS
    ws_bf = gmlp_ws[0].astype(BF16)
    bs_b = jnp.broadcast_to(gmlp_bs[0][:, :, None], (GMLP_GROUPS, CHUNK, gd)).astype(F32)
    gm = _gmlp(p, row(gmlp_norm_g[0]), ws_bf, bs_b, 256)
    gmc = _gmlp(pc, row(gmlp_norm_g[0]), ws_bf, bs_b, 256)

    k0 = 2 * gw + HEADS * 2 * QK_DIM
    v0 = k0 + HEADS * 2 * QK_DIM
    k_all = jnp.concatenate([p[:, :, k0:v0], pc[:, :, k0:v0]], axis=1)

    def v_transposed(vv):
        n = vv.shape[1]
        vt = vv.reshape(B, n, HEADS, V_DIM).transpose(0, 2, 3, 1)
        ones = jnp.ones((B, HEADS, V_ROWS - V_DIM, n), BF16)
        return jnp.concatenate([vt, ones], axis=2)

    vt_c = v_transposed(pc[:, :, v0:])
    vt_all = jnp.concatenate([v_transposed(p[:, :, v0:]), vt_c], axis=3)
    g_b = jnp.broadcast_to(subln_g[0].astype(F32)[:, None], (V_DIM, LANES))
    q_col0 = 2 * gw // LANES
    out_scale = 1.0 - lam_init
    o = _attention(lam, p, q_col0, k_all, vt_all, g_b, N, 512, out_scale)
    oc = _attention(lam, pc, q_col0, pc[:, :, k0:v0], vt_c, g_b, NC, 256, out_scale)

    w_out_bf = w_out[0].astype(BF16)
    x1, hp, lg = _proj_out(gm, o, w_out_bf, x, g1, row(norm2_g[0]), sh2, sc2, rw_pad[0], 256)
    _, hpc, lgc = _proj_out(gmc, oc, w_out_bf, ctx, cg1, row(norm2_g[0]), csh2, csc2, rw_pad[0], 256)

    T_lat = B * N
    hp_all = jnp.concatenate([hp.reshape(T_lat, -1), hpc.reshape(B * NC, -1)], axis=0)
    lg_all = jnp.concatenate([lg.reshape(T_lat, LANES), lgc.reshape(B * NC, LANES)], axis=0)
    tm_c = 128
    x = _moe(hp_all, lg_all, x1.reshape(T_lat, D), g2, T_lat, N // tm_c, tm_c, router_b[0],
             exp_gate[0], exp_up[0], exp_down[0], sh_gate[0], sh_up[0], sh_down[0],
             row(final_g), False).reshape(B, N, D)

    sh1, sc1, g1, sh2, sc2, g2 = mods(1, slice(0, B))
    x1, hp, lg = _pool_mixer(x, row(norm1_g[1]), sh1, sc1, pool_w[0].astype(BF16), row(pool_scale[0]),
                             g1, row(norm2_g[1]), sh2, sc2, rw_pad[1], 256)
    out = _moe(hp.reshape(T_lat, -1), lg.reshape(T_lat, LANES), x1.reshape(T_lat, D), g2, T_lat,
               N // tm_c, tm_c, router_b[1], exp_gate[1], exp_up[1], exp_down[1],
               sh_gate[1], sh_up[1], sh_down[1], row(final_g), True)
    return out.reshape(B, N, D)
```

```python
import functools
import math

import jax
import jax.numpy as jnp
from jax import lax
from jax.experimental import pallas as pl
from jax.experimental.pallas import tpu as pltpu

EPS = 1e-6
GRID_W = 64
ROPE_THETA = 10000.0
HEADS = 8
QK_DIM = 64
V_DIM = 128
V_ROWS = V_DIM + 16
GMLP_GROUPS = 4
CHUNK = 128
POOL_WINDOWS = (2, 4, 8, 16)
POOL_HALO = 8
N_EXPERTS = 64
TOP_K = 8
ROUTED_SCALE = 2.5
LANES = 128
EXPERT_ROWS = 256
ATTN_KEY_TILE = 1024
SAFE_SCORE_BOUND = 30.0
VMEM_LIMIT = 56 * 1024 * 1024

BF16 = jnp.bfloat16
F32 = jnp.float32
U32 = jnp.uint32
I32 = jnp.int32


def _cparams(sem):
    return pltpu.CompilerParams(dimension_semantics=sem, vmem_limit_bytes=VMEM_LIMIT)


def _dot(a, b):
    return jnp.dot(a, b, preferred_element_type=F32)


def _pack_pair(lo, hi):
    ulo = pltpu.bitcast(lo.astype(BF16).astype(F32), U32)
    uhi = pltpu.bitcast(hi.astype(BF16).astype(F32), U32)
    return (ulo >> 16) | uhi


def _unpack_pair(u):
    lo = pltpu.bitcast(u << 16, F32)
    hi = pltpu.bitcast(u & jnp.uint32(0xFFFF0000), F32)
    return lo, hi


def _rms_mod(x, g, shift, scale1p):
    y = x * lax.rsqrt(jnp.mean(x * x, axis=-1, keepdims=True) + EPS)
    return y * g * scale1p + shift


def _sigmoid(x):
    return 1.0 / (1.0 + jnp.exp(-x))


def _ada_kernel(s_ref, w_ref, b_ref, o_ref):
    o_ref[0] = _dot(s_ref[...].astype(BF16), w_ref[0].astype(BF16)) + b_ref[0]


def _ada_mod(s8, ada_w, ada_b):
    L, D, W = ada_w.shape
    tn = 768
    return pl.pallas_call(
        _ada_kernel,
        out_shape=jax.ShapeDtypeStruct((L, 8, W), F32),
        grid=(L, W // tn),
        in_specs=[pl.BlockSpec((8, D), lambda l, j: (0, 0)),
                  pl.BlockSpec((1, D, tn), lambda l, j: (l, 0, j)),
                  pl.BlockSpec((1, 1, tn), lambda l, j: (l, 0, j))],
        out_specs=pl.BlockSpec((1, 8, tn), lambda l, j: (l, 0, j)),
        compiler_params=_cparams(("arbitrary", "arbitrary")),
        name="ada_mod",
    )(s8, ada_w, ada_b.reshape(L, 1, W))


def _proj_in_kernel(x_ref, g_ref, sh_ref, sc_ref, w_ref, cq_ref, sq_ref, ck_ref, sk_ref,
                    o_ref, vt_ref, h_ref, *, q_tile, k_tile, v_tile):
    j = pl.program_id(2)

    @pl.when(j == 0)
    def _():
        h_ref[...] = _rms_mod(x_ref[0], g_ref[...], sh_ref[0], sc_ref[0]).astype(BF16)

    tm, tn = h_ref.shape[0], w_ref.shape[1]

    def project():
        return _dot(h_ref[...], w_ref[...])

    def rope(cos_ref, sin_ref):
        acc = project()
        reps = tn // LANES
        cos = jnp.concatenate([cos_ref[...]] * reps, axis=1)
        sin = jnp.concatenate([sin_ref[...]] * reps, axis=1)
        lane = lax.broadcasted_iota(I32, acc.shape, 1)
        partner = jnp.where((lane & 16) == 0,
                            pltpu.roll(acc, tn - 16, axis=1),
                            pltpu.roll(acc, 16, axis=1))
        return acc * cos + partner * sin

    is_q = j == q_tile
    is_k = j == k_tile
    is_v = j == v_tile

    @pl.when(is_q)
    def _():
        o_ref[0] = rope(cq_ref, sq_ref).astype(o_ref.dtype)

    @pl.when(is_k)
    def _():
        o_ref[0] = rope(ck_ref, sk_ref).astype(o_ref.dtype)

    @pl.when(jnp.logical_not(jnp.logical_or(jnp.logical_or(is_q, is_k), is_v)))
    def _():
        o_ref[0] = project().astype(o_ref.dtype)

    @pl.when(is_v)
    def _():
        acc = project()
        o_ref[0] = acc.astype(o_ref.dtype)
        ones = jnp.ones((V_ROWS - V_DIM, tm), vt_ref.dtype)
        for h in range(HEADS):
            vt_ref[0, h, 0:V_DIM, :] = acc[:, h * V_DIM:(h + 1) * V_DIM].T.astype(vt_ref.dtype)
            vt_ref[0, h, V_DIM:V_ROWS, :] = ones


def _proj_in(x, g, shift, scale1p, w_bf, tabs, tm):
    B, N, D = x.shape
    W = w_bf.shape[1]
    tn = 1024
    q_tile, k_tile, v_tile = 2, 3, 4
    assert W == 5 * tn and HEADS * V_DIM == tn
    kern = functools.partial(_proj_in_kernel, q_tile=q_tile, k_tile=k_tile, v_tile=v_tile)
    vec = pl.BlockSpec((1, 1, D), lambda b, i, j: (b, 0, 0))
    tab = pl.BlockSpec((tm, LANES), lambda b, i, j: (i, 0))
    return pl.pallas_call(
        kern,
        out_shape=(jax.ShapeDtypeStruct((B, N, W), BF16),
                   jax.ShapeDtypeStruct((B, HEADS, V_ROWS, N), BF16)),
        grid=(B, N // tm, W // tn),
        in_specs=[pl.BlockSpec((1, tm, D), lambda b, i, j: (b, i, 0)),
                  pl.BlockSpec((1, D), lambda b, i, j: (0, 0)),
                  vec, vec,
                  pl.BlockSpec((D, tn), lambda b, i, j: (0, j)),
                  tab, tab, tab, tab],
        out_specs=(pl.BlockSpec((1, tm, tn), lambda b, i, j: (b, i, j)),
                   pl.BlockSpec((1, HEADS, V_ROWS, tm), lambda b, i, j: (b, 0, 0, i))),
        scratch_shapes=[pltpu.VMEM((tm, D), BF16)],
        compiler_params=_cparams(("arbitrary", "arbitrary", "arbitrary")),
        name="proj_in",
    )(x, g, shift, scale1p, w_bf, *tabs)


def _gelu_tanh(x):
    return 0.5 * x * (1.0 + jnp.tanh(math.sqrt(2.0 / math.pi) * (x + 0.044715 * (x * x * x))))


def _gmlp_kernel(uv_ref, ng_ref, ws_ref, bs_ref, o_ref, *, chunks):
    gw = o_ref.shape[2]
    gd = gw // GMLP_GROUPS
    for c in range(chunks):
        rows = pl.ds(c * CHUNK, CHUNK)
        for g in range(GMLP_GROUPS):
            u = _gelu_tanh(uv_ref[0, rows, g * gd:(g + 1) * gd].astype(F32))
            v = _gelu_tanh(uv_ref[0, rows, gw + g * gd:gw + (g + 1) * gd].astype(F32))
            vn = v * lax.rsqrt(jnp.mean(v * v, axis=-1, keepdims=True) + EPS) * ng_ref[:, g * gd:(g + 1) * gd]
            mixed = _dot(ws_ref[g], vn.astype(BF16)) + bs_ref[g]
            o_ref[0, rows, g * gd:(g + 1) * gd] = (u * mixed).astype(o_ref.dtype)


def _gmlp(p, norm_g, ws_bf, bs_b, tg):
    B, N, _ = p.shape
    gw = norm_g.shape[1]
    gd = gw // GMLP_GROUPS
    kern = functools.partial(_gmlp_kernel, chunks=tg // CHUNK)
    return pl.pallas_call(
        kern,
        out_shape=jax.ShapeDtypeStruct((B, N, gw), BF16),
        grid=(B, N // tg),
        in_specs=[pl.BlockSpec((1, tg, 2 * gw), lambda b, i: (b, i, 0)),
                  pl.BlockSpec((1, gw), lambda b, i: (0, 0)),
                  pl.BlockSpec((GMLP_GROUPS, CHUNK, CHUNK), lambda b, i: (0, 0, 0)),
                  pl.BlockSpec((GMLP_GROUPS, CHUNK, gd), lambda b, i: (0, 0, 0))],
        out_specs=pl.BlockSpec((1, tg, gw), lambda b, i: (b, i, 0)),
        compiler_params=_cparams(("arbitrary", "arbitrary")),
        name="gmlp",
    )(p, norm_g, ws_bf, bs_b)


def _attn_kernel(*refs, tk, has_lat, out_scale):
    if has_lat:
        (lam_ref, q_ref, kl_ref, vl_ref, kc_ref, vc_ref, g_ref, o_ref,
         kmax_ref, acc1_ref, acc2_ref, pa1_ref, pa2_ref, pb1_ref, pb2_ref) = refs
        nl = kl_ref.shape[1] // tk
    else:
        (lam_ref, q_ref, kc_ref, vc_ref, g_ref, o_ref,
         kmax_ref, acc1_ref, acc2_ref, pa1_ref, pa2_ref, pb1_ref, pb2_ref) = refs
        nl = 0
    i = pl.program_id(2)
    tq = q_ref.shape[1]
    nc = kc_ref.shape[1]
    nt_dims = (((1,), (1,)), ((), ()))
    sub = lax.broadcasted_iota(I32, (8, LANES), 0)
    ln8 = lax.broadcasted_iota(I32, (8, LANES), 1)
    half_sel = jnp.where(jnp.logical_or(jnp.logical_and(sub == 0, ln8 < QK_DIM),
                                        jnp.logical_and(sub == 1, ln8 >= QK_DIM)), 1.0, 0.0).astype(BF16)

    def sq_norms(rows):
        rf = rows.astype(F32)
        return lax.dot_general(half_sel, (rf * rf).astype(BF16), nt_dims, preferred_element_type=F32)

    def lat_start(t):
        return t * tk if isinstance(t, int) else pl.multiple_of(t * tk, tk)

    def lat_k(t):
        return kl_ref[0, pl.ds(lat_start(t), tk), :]

    def lat_v(t):
        return vl_ref[0, 0, :, pl.ds(lat_start(t), tk)]

    def ctx_k():
        return kc_ref[0]

    def ctx_v():
        return vc_ref[0, 0]

    @pl.when(i == 0)
    def _():
        mx = jnp.max(sq_norms(ctx_k()), axis=1, keepdims=True)
        if nl:
            mx = lax.fori_loop(
                0, nl, lambda t, m: jnp.maximum(m, jnp.max(sq_norms(lat_k(t)), axis=1, keepdims=True)), mx)
        kmax_ref[...] = jnp.broadcast_to(mx, (8, LANES))

    q = q_ref[0]
    lane = lax.broadcasted_iota(I32, q.shape, 1)
    zero = jnp.zeros_like(q)
    qa = jnp.where(lane < QK_DIM, q, zero)
    qb = jnp.where(lane >= QK_DIM, q, zero)
    bound = 1.01 * jnp.sqrt(sq_norms(q) * kmax_ref[:, 0:1]) + 1e-3
    b1, b2 = bound[0:1], bound[1:2]
    safe = jnp.max(bound[0:2]) <= SAFE_SCORE_BOUND
    a0 = jnp.zeros((V_ROWS, tq), F32)

    @pl.when(safe)
    def _():
        bufs = ((pa1_ref, pa2_ref), (pb1_ref, pb2_ref))

        def qk(k, buf, rows):
            s1 = lax.dot_general(k, qa, nt_dims, preferred_element_type=F32)
            buf[0][0:rows, :] = jnp.exp2(s1 - b1).astype(BF16)
            s2 = lax.dot_general(k, qb, nt_dims, preferred_element_type=F32)
            buf[1][0:rows, :] = jnp.exp2(s2 - b2).astype(BF16)

        def pv(vt, buf, rows, a1, a2):
            a1 = a1 + _dot(vt, buf[0][0:rows, :])
            a2 = a2 + _dot(vt, buf[1][0:rows, :])
            return a1, a2

        if nl == 0:
            qk(ctx_k(), bufs[0], nc)
            a1, a2 = pv(ctx_v(), bufs[0], nc, a0, a0)
        else:
            qk(lat_k(0), bufs[0], tk)

            def pair(j, carry):
                a1, a2 = carry
                t = 2 * j
                qk(lat_k(t + 1), bufs[1], tk)
                a1, a2 = pv(lat_v(t), bufs[0], tk, a1, a2)
                qk(lat_k(t + 2), bufs[0], tk)
                return pv(lat_v(t + 1), bufs[1], tk, a1, a2)

            a1, a2 = lax.fori_loop(0, (nl - 1) // 2, pair, (a0, a0))
            if (nl - 1) % 2 == 1:
                qk(lat_k(nl - 1), bufs[1], tk)
                a1, a2 = pv(lat_v(nl - 2), bufs[0], tk, a1, a2)
                qk(ctx_k(), bufs[0], nc)
                a1, a2 = pv(lat_v(nl - 1), bufs[1], tk, a1, a2)
                a1, a2 = pv(ctx_v(), bufs[0], nc, a1, a2)
            else:
                qk(ctx_k(), bufs[1], nc)
                a1, a2 = pv(lat_v(nl - 1), bufs[0], tk, a1, a2)
                a1, a2 = pv(ctx_v(), bufs[1], nc, a1, a2)
        acc1_ref[...] = a1
        acc2_ref[...] = a2

    @pl.when(jnp.logical_not(safe))
    def _():
        def one_map(qm, k, vt, m, acc):
            s = lax.dot_general(k, qm, nt_dims, preferred_element_type=F32)
            m_new = jnp.maximum(m, jnp.max(s, axis=0, keepdims=True))
            alpha = jnp.exp2(m - m_new)
            p = jnp.exp2(s - m_new).astype(BF16)
            return m_new, alpha * acc + _dot(vt, p)

        def step(k, vt, carry):
            m1, a1, m2, a2 = carry
            m1, a1 = one_map(qa, k, vt, m1, a1)
            m2, a2 = one_map(qb, k, vt, m2, a2)
            return m1, a1, m2, a2

        m0 = jnp.full((1, tq), -jnp.inf, F32)
        carry = (m0, a0, m0, a0)
        if nl:
            carry = lax.fori_loop(0, nl, lambda t, c: step(lat_k(t), lat_v(t), c), carry)
        _, a1, _, a2 = step(ctx_k(), ctx_v(), carry)
        acc1_ref[...] = a1
        acc2_ref[...] = a2

    a1 = acc1_ref[...]
    a2 = acc2_ref[...]
    lam = lam_ref[0]
    o = a1[:V_DIM] / a1[V_DIM:V_DIM + 1] - lam * (a2[:V_DIM] / a2[V_DIM:V_DIM + 1])
    o = o * lax.rsqrt(jnp.mean(o * o, axis=0, keepdims=True) + EPS)
    reps = tq // LANES
    gain = jnp.concatenate([g_ref[...]] * reps, axis=1) if reps > 1 else g_ref[...]
    o_ref[0] = (o * gain * out_scale).T.astype(o_ref.dtype)


def _attention(lam, q_src, q_col0, k_col0, lat, ctx, g_b, tq, out_scale):
    B, nq, _ = q_src.shape
    pc, vtc = ctx
    nc = pc.shape[1]
    has_lat = lat is not None
    tk = ATTN_KEY_TILE
    assert nq % tq == 0
    head_q = pl.BlockSpec((1, tq, LANES), lambda b, h, i: (b, i, q_col0 + h))
    smem = pl.BlockSpec(memory_space=pltpu.SMEM)

    def kv_specs(n):
        return [pl.BlockSpec((1, n, LANES), lambda b, h, i: (b, 0, k_col0 + h)),
                pl.BlockSpec((1, 1, V_ROWS, n), lambda b, h, i: (b, h, 0, 0))]

    in_specs = [smem, head_q]
    args = [lam, q_src]
    if has_lat:
        p, vt = lat
        assert p.shape[1] % tk == 0
        in_specs += kv_specs(p.shape[1])
        args += [p, vt]
    in_specs += kv_specs(nc) + [pl.BlockSpec((V_DIM, LANES), lambda b, h, i: (0, 0))]
    args += [pc, vtc, g_b]
    prows = max(tk, nc) if has_lat else nc
    kern = functools.partial(_attn_kernel, tk=tk, has_lat=has_lat, out_scale=out_scale)
    return pl.pallas_call(
        kern,
        out_shape=jax.ShapeDtypeStruct((B, nq, HEADS * V_DIM), BF16),
        grid=(B, HEADS, nq // tq),
        in_specs=in_specs,
        out_specs=pl.BlockSpec((1, tq, LANES), lambda b, h, i: (b, i, h)),
        scratch_shapes=[pltpu.VMEM((8, LANES), F32),
                        pltpu.VMEM((V_ROWS, tq), F32),
                        pltpu.VMEM((V_ROWS, tq), F32)] + [pltpu.VMEM((prows, tq), BF16)] * 4,
        compiler_params=_cparams(("arbitrary", "arbitrary", "arbitrary")),
        name="diff_attn",
    )(*args)


def _mixer_epilogue(x, y, g1, n2g, sh2, sc2, rw, x1_ref, hp_ref, lg_ref):
    x1 = x + g1 * y
    x1_ref[0] = x1
    h2 = _rms_mod(x1, n2g, sh2, sc2)
    half = h2.shape[1] // 2
    hp_ref[0] = _pack_pair(h2[:, :half], h2[:, half:])
    lg_ref[0] = _dot(h2, rw)


def _proj_out_kernel(gm_ref, o_ref, w_ref, x_ref, g1_ref, n2g_ref, sh2_ref, sc2_ref, rw_ref,
                     x1_ref, hp_ref, lg_ref):
    gw = gm_ref.shape[2]
    y = _dot(gm_ref[0], w_ref[:gw, :]) + _dot(o_ref[0], w_ref[gw:, :])
    _mixer_epilogue(x_ref[0], y, g1_ref[0], n2g_ref[...], sh2_ref[0], sc2_ref[0], rw_ref[...],
                    x1_ref, hp_ref, lg_ref)


def _epilogue_out(B, N, D, tm):
    shapes = (jax.ShapeDtypeStruct((B, N, D), F32),
              jax.ShapeDtypeStruct((B, N, D // 2), U32),
              jax.ShapeDtypeStruct((B, N, LANES), F32))
    specs = (pl.BlockSpec((1, tm, D), lambda b, i: (b, i, 0)),
             pl.BlockSpec((1, tm, D // 2), lambda b, i: (b, i, 0)),
             pl.BlockSpec((1, tm, LANES), lambda b, i: (b, i, 0)))
    return shapes, specs


def _proj_out(gm, o, w_bf, x, g1, n2g, sh2, sc2, rw_pad, tm):
    B, N, D = x.shape
    gw = gm.shape[2]
    vec = pl.BlockSpec((1, 1, D), lambda b, i: (b, 0, 0))
    shapes, specs = _epilogue_out(B, N, D, tm)
    return pl.pallas_call(
        _proj_out_kernel,
        out_shape=shapes,
        grid=(B, N // tm),
        in_specs=[pl.BlockSpec((1, tm, gw), lambda b, i: (b, i, 0)),
                  pl.BlockSpec((1, tm, o.shape[2]), lambda b, i: (b, i, 0)),
                  pl.BlockSpec(w_bf.shape, lambda b, i: (0, 0)),
                  pl.BlockSpec((1, tm, D), lambda b, i: (b, i, 0)),
                  vec,
                  pl.BlockSpec((1, D), lambda b, i: (0, 0)),
                  vec, vec,
                  pl.BlockSpec((D, LANES), lambda b, i: (0, 0))],
        out_specs=specs,
        compiler_params=_cparams(("arbitrary", "arbitrary")),
        name="proj_out",
    )(gm, o, w_bf, x, g1, n2g, sh2, sc2, rw_pad)


def _pool_kernel(x_ref, xp_ref, xn_ref, n1g_ref, sh1_ref, sc1_ref, pw_ref, ps_ref,
                 g1_ref, n2g_ref, sh2_ref, sc2_ref, rw_ref, x1_ref, hp_ref, lg_ref, y_ref, *, seq):
    i = pl.program_id(1)
    last = pl.num_programs(1) - 1
    x = x_ref[0]
    tm, D = x.shape
    n1g, sh1, sc1 = n1g_ref[...], sh1_ref[0], sc1_ref[0]
    hc = _rms_mod(x, n1g, sh1, sc1)
    hp = _rms_mod(xp_ref[0], n1g, sh1, sc1) * (i > 0).astype(F32)
    hn = _rms_mod(xn_ref[0], n1g, sh1, sc1) * (i < last).astype(F32)
    he = jnp.concatenate([hp, hc, hn], axis=0).astype(BF16)
    groups = len(POOL_WINDOWS)
    gd = D // groups
    row = lax.broadcasted_iota(I32, (tm, tm + 2 * POOL_HALO), 0)
    col = lax.broadcasted_iota(I32, (tm, tm + 2 * POOL_HALO), 1)
    tglob = i * tm + lax.broadcasted_iota(I32, (tm, 1), 0)
    for g, w in enumerate(POOL_WINDOWS):
        lo = row + (POOL_HALO - w // 2)
        band = jnp.where(jnp.logical_and(col >= lo, col < lo + w), 1.0, 0.0).astype(BF16)
        s = _dot(band, he[:, g * gd:(g + 1) * gd])
        cnt = jnp.minimum(tglob + w // 2, seq) - jnp.maximum(tglob - w // 2, 0)
        pooled = s / cnt.astype(F32) - hc[:, g * gd:(g + 1) * gd]
        y_ref[:, g * gd:(g + 1) * gd] = _dot(pooled.astype(BF16), pw_ref[g])
    y = y_ref[...] * ps_ref[...]
    _mixer_epilogue(x, y, g1_ref[0], n2g_ref[...], sh2_ref[0], sc2_ref[0], rw_ref[...],
                    x1_ref, hp_ref, lg_ref)


def _pool_mixer(x, n1g, sh1, sc1, pw_bf, ps, g1, n2g, sh2, sc2, rw_pad, tm):
    B, N, D = x.shape
    gd = D // len(POOL_WINDOWS)
    hb = tm // POOL_HALO
    nhb = N // POOL_HALO
    vec = pl.BlockSpec((1, 1, D), lambda b, i: (b, 0, 0))
    row = pl.BlockSpec((1, D), lambda b, i: (0, 0))
    shapes, specs = _epilogue_out(B, N, D, tm)
    kern = functools.partial(_pool_kernel, seq=N)
    return pl.pallas_call(
        kern,
        out_shape=shapes,
        grid=(B, N // tm),
        in_specs=[pl.BlockSpec((1, tm, D), lambda b, i: (b, i, 0)),
                  pl.BlockSpec((1, POOL_HALO, D), lambda b, i: (b, jnp.maximum(i * hb - 1, 0), 0)),
                  pl.BlockSpec((1, POOL_HALO, D), lambda b, i: (b, jnp.minimum((i + 1) * hb, nhb - 1), 0)),
                  row, vec, vec,
                  pl.BlockSpec((len(POOL_WINDOWS), gd, gd), lambda b, i: (0, 0, 0)),
                  row, vec, row, vec, vec,
                  pl.BlockSpec((D, LANES), lambda b, i: (0, 0))],
        out_specs=specs,
        scratch_shapes=[pltpu.VMEM((tm, D), F32)],
        compiler_params=_cparams(("arbitrary", "arbitrary")),
        name="pool_mixer",
    )(x, x, x, n1g, sh1, sc1, pw_bf, ps, g1, n2g, sh2, sc2, rw_pad)


def _route_kernel(lg_ref, b_ref, idx_ref, w_ref, cnt_ref, carry_ref):
    @pl.when(pl.program_id(0) == 0)
    def _():
        carry_ref[...] = jnp.zeros_like(carry_ref)

    logits = lg_ref[...]
    tr = logits.shape[0]
    scores = _sigmoid(logits)
    sel = scores + b_ref[...]
    lane = lax.broadcasted_iota(I32, (tr, LANES), 1)
    picked = jnp.zeros((tr, LANES), F32)
    idx_o = jnp.zeros((tr, LANES), I32)
    w_o = jnp.zeros((tr, LANES), F32)
    wsum = jnp.zeros((tr, 1), F32)
    for k in range(TOP_K):
        mx = jnp.max(sel, axis=1, keepdims=True)
        first = jnp.min(jnp.where(sel == mx, lane, LANES), axis=1, keepdims=True)
        oh = lane == first
        wk = jnp.sum(jnp.where(oh, scores, 0.0), axis=1, keepdims=True)
        wsum = wsum + wk
        sel = jnp.where(oh, -jnp.inf, sel)
        picked = jnp.where(oh, 1.0, picked)
        idx_o = jnp.where(lane == k, first, idx_o)
        w_o = jnp.where(lane == k, wk, w_o)
    idx_ref[...] = idx_o
    w_ref[...] = w_o / wsum * ROUTED_SCALE
    carry_ref[...] = carry_ref[...] + jnp.sum(picked, axis=0, keepdims=True)
    cnt_ref[...] = carry_ref[...]


def _route(logits, bias_pad, tr):
    T = logits.shape[0]
    blk = pl.BlockSpec((tr, LANES), lambda i: (i, 0))
    one = pl.BlockSpec((1, LANES), lambda i: (0, 0))
    return pl.pallas_call(
        _route_kernel,
        out_shape=(jax.ShapeDtypeStruct((T, LANES), I32),
                   jax.ShapeDtypeStruct((T, LANES), F32),
                   jax.ShapeDtypeStruct((1, LANES), F32)),
        grid=(T // tr,),
        in_specs=[blk, one],
        out_specs=(blk, blk, one),
        scratch_shapes=[pltpu.VMEM((1, LANES), F32)],
        compiler_params=_cparams(("arbitrary",)),
        name="route",
    )(logits, bias_pad)


def _experts_kernel(be_ref, bv_ref, tok_next_ref, tok_first_ref, dst_prev_ref, hp_ref,
                    wg_ref, wu_ref, wd_ref, o8_ref, wgb, wub, wdb, xb0, xb1, yb0, yb1, gsem, ssem):
    i = pl.program_id(0)
    rows = xb0.shape[0]
    valid = bv_ref[i] != 0
    prev_valid = jnp.logical_and(i > 0, bv_ref[jnp.maximum(i - 1, 0)] != 0)
    fresh = jnp.logical_or(i == 0, be_ref[i] != be_ref[jnp.maximum(i - 1, 0)])
    xbufs, ybufs = (xb0, xb1), (yb0, yb1)

    def gather(tok_ref, slot):
        return [pltpu.make_async_copy(hp_ref.at[pl.ds(tok_ref[0, 0, r], 1), :],
                                      xbufs[slot].at[pl.ds(r, 1), :], gsem.at[slot]) for r in range(rows)]

    def gather_wait(slot):
        for r in range(rows):
            pltpu.make_async_copy(hp_ref.at[pl.ds(0, 1), :], xbufs[slot].at[pl.ds(r, 1), :],
                                  gsem.at[slot]).wait()

    def scatter(slot):
        return [pltpu.make_async_copy(ybufs[slot].at[pl.ds(r, 1), :],
                                      o8_ref.at[pl.ds(dst_prev_ref[0, 0, r], 1), :], ssem) for r in range(rows)]

    def start_all(copies):
        for r, cp in enumerate(copies):
            cp.start(priority=r % 2)

    def compute(slot):
        lo, hi = _unpack_pair(xbufs[slot][...])
        lo, hi = lo.astype(BF16), hi.astype(BF16)
        half = lo.shape[1]
        g = _dot(lo, wgb[:half, :]) + _dot(hi, wgb[half:, :])
        u = _dot(lo, wub[:half, :]) + _dot(hi, wub[half:, :])
        a = (g * _sigmoid(g) * u).astype(BF16)
        y = _dot(a, wdb[...])
        ybufs[slot][...] = _pack_pair(y[:, :half], y[:, half:])

    @pl.when(jnp.logical_and(valid, fresh))
    def _():
        wgb[...] = wg_ref[0].astype(BF16)
        wub[...] = wu_ref[0].astype(BF16)
        wdb[...] = wd_ref[0].astype(BF16)

    @pl.when(jnp.logical_and(valid, i == 0))
    def _():
        start_all(gather(tok_first_ref, 0))
        gather_wait(0)
        start_all(gather(tok_next_ref, 1))
        compute(0)
        gather_wait(1)

    for par in (0, 1):
        @pl.when(jnp.logical_and(jnp.logical_and(valid, i > 0), i % 2 == par))
        def _():
            start_all(gather(tok_next_ref, 1 - par))
            out_copies = scatter(1 - par)
            start_all(out_copies)
            compute(par)
            for cp in out_copies:
                cp.wait()
            gather_wait(1 - par)

        @pl.when(jnp.logical_and(jnp.logical_and(jnp.logical_not(valid), prev_valid), i % 2 == par))
        def _():
            out_copies = scatter(1 - par)
            start_all(out_copies)
            for cp in out_copies:
                cp.wait()


def _experts(block_e, block_valid, tok3, dst3, hp, wg, wu, wd, out_rows):
    T, W = hp.shape
    nb, _, R = tok3.shape
    _, D, ED = wg.shape
    smem_blk = lambda f: pl.BlockSpec((1, 1, R), f, memory_space=pltpu.SMEM)
    grid_spec = pltpu.PrefetchScalarGridSpec(
        num_scalar_prefetch=2,
        grid=(nb,),
        in_specs=[smem_blk(lambda i, be, bv: (jnp.minimum(i + 1, nb - 1), 0, 0)),
                  smem_blk(lambda i, be, bv: (0, 0, 0)),
                  smem_blk(lambda i, be, bv: (jnp.maximum(i - 1, 0), 0, 0)),
                  pl.BlockSpec(memory_space=pl.ANY),
                  pl.BlockSpec((1, D, ED), lambda i, be, bv: (be[i], 0, 0)),
                  pl.BlockSpec((1, D, ED), lambda i, be, bv: (be[i], 0, 0)),
                  pl.BlockSpec((1, ED, D), lambda i, be, bv: (be[i], 0, 0))],
        out_specs=pl.BlockSpec(memory_space=pl.ANY),
        scratch_shapes=[pltpu.VMEM((D, ED), BF16), pltpu.VMEM((D, ED), BF16), pltpu.VMEM((ED, D), BF16),
                        pltpu.VMEM((R, W), U32), pltpu.VMEM((R, W), U32),
                        pltpu.VMEM((R, W), U32), pltpu.VMEM((R, W), U32),
                        pltpu.SemaphoreType.DMA((2,)), pltpu.SemaphoreType.DMA(())],
    )
    return pl.pallas_call(
        _experts_kernel,
        out_shape=jax.ShapeDtypeStruct((out_rows, W), U32),
        grid_spec=grid_spec,
        compiler_params=_cparams(("arbitrary",)),
        name="experts",
    )(block_e, block_valid, tok3, tok3, dst3, hp, wg, wu, wd)


def _combine_kernel(*refs, final_norm):
    ys = refs[:TOP_K]
    wts_ref, hp_ref, x_ref, g2_ref, sg_ref, su_ref, sd_ref, fg_ref, o_ref = refs[TOP_K:]
    lo, hi = _unpack_pair(hp_ref[...])
    lo, hi = lo.astype(BF16), hi.astype(BF16)
    half = lo.shape[1]
    g = _dot(lo, sg_ref[:half, :]) + _dot(hi, sg_ref[half:, :])
    u = _dot(lo, su_ref[:half, :]) + _dot(hi, su_ref[half:, :])
    shared = _dot((g * _sigmoid(g) * u).astype(BF16), sd_ref[...])
    wts = wts_ref[...]
    acc_lo = shared[:, :half]
    acc_hi = shared[:, half:]
    for k in range(TOP_K):
        ylo, yhi = _unpack_pair(ys[k][...])
        wk = wts[:, k:k + 1]
        acc_lo = acc_lo + wk * ylo
        acc_hi = acc_hi + wk * yhi
    g2 = g2_ref[0]
    x = x_ref[...]
    out_lo = x[:, :half] + g2[:, :half] * acc_lo
    out_hi = x[:, half:] + g2[:, half:] * acc_hi
    if final_norm:
        ms = (jnp.sum(out_lo * out_lo, axis=-1, keepdims=True)
              + jnp.sum(out_hi * out_hi, axis=-1, keepdims=True)) / (2 * half)
        inv = lax.rsqrt(ms + EPS)
        fg = fg_ref[...]
        out_lo = out_lo * inv * fg[:, :half]
        out_hi = out_hi * inv * fg[:, half:]
    o_ref[:, :half] = out_lo
    o_ref[:, half:] = out_hi


def _combine(o8, wts, hp, x1, g2, sg_bf, su_bf, sd_bf, fg, tm, tiles_per_batch, final_norm):
    T, D = x1.shape
    W = D // 2
    ED = sg_bf.shape[1]
    nt = T // tm
    kern = functools.partial(_combine_kernel, final_norm=final_norm)
    slot_specs = [pl.BlockSpec((tm, W), functools.partial(lambda i, k: (k * nt + i, 0), k=k)) for k in range(TOP_K)]
    return pl.pallas_call(
        kern,
        out_shape=jax.ShapeDtypeStruct((T, D), F32),
        grid=(nt,),
        in_specs=slot_specs + [
            pl.BlockSpec((tm, LANES), lambda i: (i, 0)),
            pl.BlockSpec((tm, W), lambda i: (i, 0)),
            pl.BlockSpec((tm, D), lambda i: (i, 0)),
            pl.BlockSpec((1, 1, D), lambda i: (i // tiles_per_batch, 0, 0)),
            pl.BlockSpec((D, ED), lambda i: (0, 0)),
            pl.BlockSpec((D, ED), lambda i: (0, 0)),
            pl.BlockSpec((ED, D), lambda i: (0, 0)),
            pl.BlockSpec((1, D), lambda i: (0, 0))],
        out_specs=pl.BlockSpec((tm, D), lambda i: (i, 0)),
        compiler_params=_cparams(("arbitrary",)),
        name="combine",
    )(*([o8] * TOP_K), wts, hp, x1, g2, sg_bf, su_bf, sd_bf, fg)


def _moe(hp, logits, x1, g2, n_out, tiles_per_batch, tm, router_b, wg, wu, wd, sg, su, sd, fg, final_norm):
    T = hp.shape[0]
    R = EXPERT_ROWS
    A = T * TOP_K
    bias_pad = jnp.full((1, LANES), -1e30, F32).at[0, :N_EXPERTS].set(router_b.astype(F32))
    idx, wts, counts = _route(logits, bias_pad, 256)
    counts = counts[0, :N_EXPERTS].astype(I32)
    padded = (counts + R - 1) // R * R
    pend = jnp.cumsum(padded)
    pstart = pend - padded
    start = jnp.cumsum(counts) - counts
    order = jnp.argsort(idx[:, :TOP_K].reshape(-1), stable=True).astype(I32)
    n_blocks = -(-A // R) + N_EXPERTS + 1
    bstart = jnp.arange(n_blocks, dtype=I32) * R
    block_e = jnp.minimum(jnp.sum((pend[None, :] <= bstart[:, None]).astype(I32), axis=1), N_EXPERTS - 1)
    block_valid = (bstart < pend[-1]).astype(I32)
    rows = jnp.arange(n_blocks * R, dtype=I32)
    e_r = jnp.repeat(block_e, R, total_repeat_length=n_blocks * R)
    j_r = rows - pstart[e_r]
    valid_r = jnp.logical_and(j_r < counts[e_r], rows < pend[-1])
    a_r = order[jnp.clip(start[e_r] + j_r, 0, A - 1)]
    t_r = a_r // TOP_K
    tok_r = jnp.where(valid_r, t_r, 0)
    dst_r = jnp.where(jnp.logical_and(valid_r, t_r < n_out), (a_r % TOP_K) * n_out + t_r,
                      TOP_K * n_out + rows % R)
    o8 = _experts(block_e, block_valid, tok_r.reshape(n_blocks, 1, R), dst_r.reshape(n_blocks, 1, R),
                  hp, wg, wu, wd, TOP_K * n_out + R)
    return _combine(o8, wts, hp, x1, g2, sg.astype(BF16), su.astype(BF16), sd.astype(BF16),
                    fg, tm, tiles_per_batch, final_norm)


def _rope_tables(n):
    rows = n // GRID_W
    row = jnp.repeat(jnp.arange(rows, dtype=F32), GRID_W)
    col = jnp.tile(jnp.arange(GRID_W, dtype=F32), rows)
    half = QK_DIM // 2
    inv = 1.0 / (ROPE_THETA ** (jnp.arange(0, half, 2, dtype=F32) / half))
    ang_r = row[:, None] * inv[None, :]
    ang_c = col[:, None] * inv[None, :]
    cr, sr, cc, sn = jnp.cos(ang_r), jnp.sin(ang_r), jnp.cos(ang_c), jnp.sin(ang_c)
    cos64 = jnp.concatenate([cr, cr, cc, cc], axis=1)
    sin64 = jnp.concatenate([-sr, sr, -sn, sn], axis=1)
    reps = LANES // QK_DIM
    return jnp.tile(cos64, (1, reps)), jnp.tile(sin64, (1, reps))


def kernel(x, c, ctx, c_ctx, ada_w, ada_b, norm1_g, norm2_g, w_in, w_out, gmlp_norm_g, gmlp_ws, gmlp_bs,
           lam_q1, lam_k1, lam_q2, lam_k2, subln_g, pool_w, pool_scale, router_w, router_b, exp_gate,
           exp_up, exp_down, sh_gate, sh_up, sh_down, final_g):
    B, N, D = x.shape
    NC = ctx.shape[1]
    depth = ada_w.shape[0]
    assert depth == 2 and B + 1 <= 8

    s8 = jnp.zeros((8, D), F32).at[:B].set(jax.nn.silu(c)).at[B].set(jax.nn.silu(c_ctx))
    mod = _ada_mod(s8, ada_w, ada_b).reshape(depth, 8, 6, D)

    def mods(layer, rows):
        m = mod[layer, rows]
        sh1, s1, g1, sh2, s2, g2 = (m[:, k][:, None, :] for k in range(6))
        return sh1, 1.0 + s1, g1, sh2, 1.0 + s2, g2

    rw_pad = [jnp.zeros((D, LANES), F32).at[:, :N_EXPERTS].set(router_w[l]) for l in range(depth)]
    row = lambda v: v.reshape(1, -1).astype(F32)

    sh1, sc1, g1, sh2, sc2, g2 = mods(0, slice(0, B))
    csh1, csc1, cg1, csh2, csc2, _ = (jnp.broadcast_to(v, (B, 1, D)) for v in mods(0, slice(B, B + 1)))
    lam_init = 0.8 - 0.6 * math.exp(-0.3 * 0)
    lam = (jnp.exp(jnp.sum(lam_q1[0] * lam_k1[0])) - jnp.exp(jnp.sum(lam_q2[0] * lam_k2[0]))
           + lam_init).reshape(1).astype(F32)
    w_in_bf = w_in[0].astype(BF16)
    cos_t, sin_t = _rope_tables(N)
    qscale = QK_DIM ** -0.5 * math.log2(math.e)
    tabs = (cos_t * qscale, sin_t * qscale, cos_t, sin_t)
    ones_c = jnp.ones((NC, LANES), F32)
    zeros_c = jnp.zeros((NC, LANES), F32)
    tabs_c = (ones_c * qscale, zeros_c, ones_c, zeros_c)

    p, vt = _proj_in(x, row(norm1_g[0]), sh1, sc1, w_in_bf, tabs, min(512, N))
    pc, vtc = _proj_in(ctx, row(norm1_g[0]), csh1, csc1, w_in_bf, tabs_c, min(512, NC))

    gw = gmlp_norm_g.shape[1]
    gd = gw // GMLP_GROUPS
    ws_bf = gmlp_ws[0].astype(BF16)
    bs_b = jnp.broadcast_to(gmlp_bs[0][:, :, None], (GMLP_GROUPS, CHUNK, gd)).astype(F32)
    gm = _gmlp(p, row(gmlp_norm_g[0]), ws_bf, bs_b, 256)
    gmc = _gmlp(pc, row(gmlp_norm_g[0]), ws_bf, bs_b, 256)

    g_b = jnp.broadcast_to(subln_g[0].astype(F32)[:, None], (V_DIM, LANES))
    q_col0 = 2 * gw // LANES
    k_col0 = q_col0 + HEADS * 2 * QK_DIM // LANES
    out_scale = 1.0 - lam_init
    o = _attention(lam, p, q_col0, k_col0, (p, vt), (pc, vtc), g_b, min(512, N), out_scale)
    oc = _attention(lam, pc, q_col0, k_col0, None, (pc, vtc), g_b, NC, out_scale)

    w_out_bf = w_out[0].astype(BF16)
    x1, hp, lg = _proj_out(gm, o, w_out_bf, x, g1, row(norm2_g[0]), sh2, sc2, rw_pad[0], 256)
    _, hpc, lgc = _proj_out(gmc, oc, w_out_bf, ctx, cg1, row(norm2_g[0]), csh2, csc2, rw_pad[0], 256)

    T_lat = B * N
    hp_all = jnp.concatenate([hp.reshape(T_lat, -1), hpc.reshape(B * NC, -1)], axis=0)
    lg_all = jnp.concatenate([lg.reshape(T_lat, LANES), lgc.reshape(B * NC, LANES)], axis=0)
    tm_c = 128
    x = _moe(hp_all, lg_all, x1.reshape(T_lat, D), g2, T_lat, N // tm_c, tm_c, router_b[0],
             exp_gate[0], exp_up[0], exp_down[0], sh_gate[0], sh_up[0], sh_down[0],
             row(final_g), False).reshape(B, N, D)

    sh1, sc1, g1, sh2, sc2, g2 = mods(1, slice(0, B))
    x1, hp, lg = _pool_mixer(x, row(norm1_g[1]), sh1, sc1, pool_w[0].astype(BF16), row(pool_scale[0]),
                             g1, row(norm2_g[1]), sh2, sc2, rw_pad[1], 256)
    out = _moe(hp.reshape(T_lat, -1), lg.reshape(T_lat, LANES), x1.reshape(T_lat, D), g2, T_lat,
               N // tm_c, tm_c, router_b[1], exp_gate[1], exp_up[1], exp_down[1],
               sh_gate[1], sh_up[1], sh_down[1], row(final_g), True)
    return out.reshape(B, N, D)
```

```python
import functools
import math

import jax
import jax.numpy as jnp
from jax import lax
from jax.experimental import pallas as pl
from jax.experimental.pallas import tpu as pltpu

EPS = 1e-6
GRID_W = 64
ROPE_THETA = 10000.0
HEADS = 8
QK_DIM = 64
V_DIM = 128
V_ROWS = V_DIM + 16
GMLP_GROUPS = 4
CHUNK = 128
POOL_WINDOWS = (2, 4, 8, 16)
POOL_HALO = 8
N_EXPERTS = 64
TOP_K = 8
ROUTED_SCALE = 2.5
LANES = 128
EXPERT_ROWS = 256
ATTN_KEY_TILE = 1024
GATHER_PRIORITY = 0
SCATTER_PRIORITY = 1
SAFE_SCORE_BOUND = 30.0
VMEM_LIMIT = 56 * 1024 * 1024

BF16 = jnp.bfloat16
F32 = jnp.float32
U32 = jnp.uint32
I32 = jnp.int32


def _cparams(sem):
    return pltpu.CompilerParams(dimension_semantics=sem, vmem_limit_bytes=VMEM_LIMIT)


def _dot(a, b):
    return jnp.dot(a, b, preferred_element_type=F32)


def _pack_pair(lo, hi):
    ulo = pltpu.bitcast(lo.astype(BF16).astype(F32), U32)
    uhi = pltpu.bitcast(hi.astype(BF16).astype(F32), U32)
    return (ulo >> 16) | uhi


def _unpack_pair(u):
    lo = pltpu.bitcast(u << 16, F32)
    hi = pltpu.bitcast(u & jnp.uint32(0xFFFF0000), F32)
    return lo, hi


def _rms_mod(x, g, shift, scale1p):
    y = x * lax.rsqrt(jnp.mean(x * x, axis=-1, keepdims=True) + EPS)
    return y * g * scale1p + shift


def _sigmoid(x):
    return 1.0 / (1.0 + jnp.exp(-x))


def _ada_kernel(s_ref, w_ref, b_ref, o_ref):
    o_ref[0] = _dot(s_ref[...].astype(BF16), w_ref[0].astype(BF16)) + b_ref[0]


def _ada_mod(s8, ada_w, ada_b):
    L, D, W = ada_w.shape
    tn = 768
    return pl.pallas_call(
        _ada_kernel,
        out_shape=jax.ShapeDtypeStruct((L, 8, W), F32),
        grid=(L, W // tn),
        in_specs=[pl.BlockSpec((8, D), lambda l, j: (0, 0)),
                  pl.BlockSpec((1, D, tn), lambda l, j: (l, 0, j)),
                  pl.BlockSpec((1, 1, tn), lambda l, j: (l, 0, j))],
        out_specs=pl.BlockSpec((1, 8, tn), lambda l, j: (l, 0, j)),
        compiler_params=_cparams(("arbitrary", "arbitrary")),
        name="ada_mod",
    )(s8, ada_w, ada_b.reshape(L, 1, W))


def _proj_in_kernel(x_ref, g_ref, sh_ref, sc_ref, w_ref, cq_ref, sq_ref, ck_ref, sk_ref,
                    o_ref, vt_ref, h_ref, *, q_tile, k_tile, v_tile):
    j = pl.program_id(2)

    @pl.when(j == 0)
    def _():
        h_ref[...] = _rms_mod(x_ref[0], g_ref[...], sh_ref[0], sc_ref[0]).astype(BF16)

    tm, tn = h_ref.shape[0], w_ref.shape[1]

    def project():
        return _dot(h_ref[...], w_ref[...])

    def rope(cos_ref, sin_ref):
        acc = project()
        reps = tn // LANES
        cos = jnp.concatenate([cos_ref[...]] * reps, axis=1)
        sin = jnp.concatenate([sin_ref[...]] * reps, axis=1)
        lane = lax.broadcasted_iota(I32, acc.shape, 1)
        partner = jnp.where((lane & 16) == 0,
                            pltpu.roll(acc, tn - 16, axis=1),
                            pltpu.roll(acc, 16, axis=1))
        return acc * cos + partner * sin

    is_q = j == q_tile
    is_k = j == k_tile
    is_v = j == v_tile

    @pl.when(is_q)
    def _():
        o_ref[0] = rope(cq_ref, sq_ref).astype(o_ref.dtype)

    @pl.when(is_k)
    def _():
        o_ref[0] = rope(ck_ref, sk_ref).astype(o_ref.dtype)

    @pl.when(jnp.logical_not(jnp.logical_or(jnp.logical_or(is_q, is_k), is_v)))
    def _():
        o_ref[0] = project().astype(o_ref.dtype)

    @pl.when(is_v)
    def _():
        acc = project()
        o_ref[0] = acc.astype(o_ref.dtype)
        ones = jnp.ones((V_ROWS - V_DIM, tm), vt_ref.dtype)
        for h in range(HEADS):
            vt_ref[0, h, 0:V_DIM, :] = acc[:, h * V_DIM:(h + 1) * V_DIM].T.astype(vt_ref.dtype)
            vt_ref[0, h, V_DIM:V_ROWS, :] = ones


def _proj_in(x, g, shift, scale1p, w_bf, tabs, tm):
    B, N, D = x.shape
    W = w_bf.shape[1]
    tn = 1024
    q_tile, k_tile, v_tile = 2, 3, 4
    assert W == 5 * tn and HEADS * V_DIM == tn
    kern = functools.partial(_proj_in_kernel, q_tile=q_tile, k_tile=k_tile, v_tile=v_tile)
    vec = pl.BlockSpec((1, 1, D), lambda b, i, j: (b, 0, 0))
    tab = pl.BlockSpec((tm, LANES), lambda b, i, j: (i, 0))
    return pl.pallas_call(
        kern,
        out_shape=(jax.ShapeDtypeStruct((B, N, W), BF16),
                   jax.ShapeDtypeStruct((B, HEADS, V_ROWS, N), BF16)),
        grid=(B, N // tm, W // tn),
        in_specs=[pl.BlockSpec((1, tm, D), lambda b, i, j: (b, i, 0)),
                  pl.BlockSpec((1, D), lambda b, i, j: (0, 0)),
                  vec, vec,
                  pl.BlockSpec((D, tn), lambda b, i, j: (0, j)),
                  tab, tab, tab, tab],
        out_specs=(pl.BlockSpec((1, tm, tn), lambda b, i, j: (b, i, j)),
                   pl.BlockSpec((1, HEADS, V_ROWS, tm), lambda b, i, j: (b, 0, 0, i))),
        scratch_shapes=[pltpu.VMEM((tm, D), BF16)],
        compiler_params=_cparams(("arbitrary", "arbitrary", "arbitrary")),
        name="proj_in",
    )(x, g, shift, scale1p, w_bf, *tabs)


def _gelu_tanh(x):
    return 0.5 * x * (1.0 + jnp.tanh(math.sqrt(2.0 / math.pi) * (x + 0.044715 * (x * x * x))))


def _gmlp_kernel(uv_ref, ng_ref, ws_ref, bs_ref, o_ref, *, chunks):
    gw = o_ref.shape[2]
    gd = gw // GMLP_GROUPS
    for c in range(chunks):
        rows = pl.ds(c * CHUNK, CHUNK)
        for g in range(GMLP_GROUPS):
            u = _gelu_tanh(uv_ref[0, rows, g * gd:(g + 1) * gd].astype(F32))
            v = _gelu_tanh(uv_ref[0, rows, gw + g * gd:gw + (g + 1) * gd].astype(F32))
            vn = v * lax.rsqrt(jnp.mean(v * v, axis=-1, keepdims=True) + EPS) * ng_ref[:, g * gd:(g + 1) * gd]
            mixed = _dot(ws_ref[g], vn.astype(BF16)) + bs_ref[g]
            o_ref[0, rows, g * gd:(g + 1) * gd] = (u * mixed).astype(o_ref.dtype)


def _gmlp(p, norm_g, ws_bf, bs_b, tg):
    B, N, _ = p.shape
    gw = norm_g.shape[1]
    gd = gw // GMLP_GROUPS
    kern = functools.partial(_gmlp_kernel, chunks=tg // CHUNK)
    return pl.pallas_call(
        kern,
        out_shape=jax.ShapeDtypeStruct((B, N, gw), BF16),
        grid=(B, N // tg),
        in_specs=[pl.BlockSpec((1, tg, 2 * gw), lambda b, i: (b, i, 0)),
                  pl.BlockSpec((1, gw), lambda b, i: (0, 0)),
                  pl.BlockSpec((GMLP_GROUPS, CHUNK, CHUNK), lambda b, i: (0, 0, 0)),
                  pl.BlockSpec((GMLP_GROUPS, CHUNK, gd), lambda b, i: (0, 0, 0))],
        out_specs=pl.BlockSpec((1, tg, gw), lambda b, i: (b, i, 0)),
        compiler_params=_cparams(("arbitrary", "arbitrary")),
        name="gmlp",
    )(p, norm_g, ws_bf, bs_b)


def _attn_kernel(*refs, tk, has_lat, out_scale):
    if has_lat:
        (lam_ref, q_ref, kl_ref, vl_ref, kc_ref, vc_ref, g_ref, o_ref,
         kmax_ref, acc1_ref, acc2_ref, pa1_ref, pa2_ref, pb1_ref, pb2_ref) = refs
        nl = kl_ref.shape[1] // tk
    else:
        (lam_ref, q_ref, kc_ref, vc_ref, g_ref, o_ref,
         kmax_ref, acc1_ref, acc2_ref, pa1_ref, pa2_ref, pb1_ref, pb2_ref) = refs
        nl = 0
    i = pl.program_id(2)
    tq = q_ref.shape[1]
    nc = kc_ref.shape[1]
    nt_dims = (((1,), (1,)), ((), ()))
    sub = lax.broadcasted_iota(I32, (8, LANES), 0)
    ln8 = lax.broadcasted_iota(I32, (8, LANES), 1)
    half_sel = jnp.where(jnp.logical_or(jnp.logical_and(sub == 0, ln8 < QK_DIM),
                                        jnp.logical_and(sub == 1, ln8 >= QK_DIM)), 1.0, 0.0).astype(BF16)

    def sq_norms(rows):
        rf = rows.astype(F32)
        return lax.dot_general(half_sel, (rf * rf).astype(BF16), nt_dims, preferred_element_type=F32)

    def lat_start(t):
        return t * tk if isinstance(t, int) else pl.multiple_of(t * tk, tk)

    def lat_k(t):
        return kl_ref[0, pl.ds(lat_start(t), tk), :]

    def lat_v(t):
        return vl_ref[0, 0, :, pl.ds(lat_start(t), tk)]

    def ctx_k():
        return kc_ref[0]

    def ctx_v():
        return vc_ref[0, 0]

    @pl.when(i == 0)
    def _():
        mx = jnp.max(sq_norms(ctx_k()), axis=1, keepdims=True)
        if nl:
            mx = lax.fori_loop(
                0, nl, lambda t, m: jnp.maximum(m, jnp.max(sq_norms(lat_k(t)), axis=1, keepdims=True)), mx)
        kmax_ref[...] = jnp.broadcast_to(mx, (8, LANES))

    q = q_ref[0]
    lane = lax.broadcasted_iota(I32, q.shape, 1)
    zero = jnp.zeros_like(q)
    qa = jnp.where(lane < QK_DIM, q, zero)
    qb = jnp.where(lane >= QK_DIM, q, zero)
    bound = 1.01 * jnp.sqrt(sq_norms(q) * kmax_ref[:, 0:1]) + 1e-3
    b1, b2 = bound[0:1], bound[1:2]
    safe = jnp.max(bound[0:2]) <= SAFE_SCORE_BOUND
    a0 = jnp.zeros((V_ROWS, tq), F32)

    @pl.when(safe)
    def _():
        bufs = ((pa1_ref, pa2_ref), (pb1_ref, pb2_ref))

        def qk(k, buf, rows):
            s1 = lax.dot_general(k, qa, nt_dims, preferred_element_type=F32)
            buf[0][0:rows, :] = jnp.exp2(s1 - b1).astype(BF16)
            s2 = lax.dot_general(k, qb, nt_dims, preferred_element_type=F32)
            buf[1][0:rows, :] = jnp.exp2(s2 - b2).astype(BF16)

        def pv(vt, buf, rows, a1, a2):
            a1 = a1 + _dot(vt, buf[0][0:rows, :])
            a2 = a2 + _dot(vt, buf[1][0:rows, :])
            return a1, a2

        if nl == 0:
            qk(ctx_k(), bufs[0], nc)
            a1, a2 = pv(ctx_v(), bufs[0], nc, a0, a0)
        else:
            qk(lat_k(0), bufs[0], tk)

            def pair(j, carry):
                a1, a2 = carry
                t = 2 * j
                qk(lat_k(t + 1), bufs[1], tk)
                a1, a2 = pv(lat_v(t), bufs[0], tk, a1, a2)
                qk(lat_k(t + 2), bufs[0], tk)
                return pv(lat_v(t + 1), bufs[1], tk, a1, a2)

            a1, a2 = lax.fori_loop(0, (nl - 1) // 2, pair, (a0, a0))
            if (nl - 1) % 2 == 1:
                qk(lat_k(nl - 1), bufs[1], tk)
                a1, a2 = pv(lat_v(nl - 2), bufs[0], tk, a1, a2)
                qk(ctx_k(), bufs[0], nc)
                a1, a2 = pv(lat_v(nl - 1), bufs[1], tk, a1, a2)
                a1, a2 = pv(ctx_v(), bufs[0], nc, a1, a2)
            else:
                qk(ctx_k(), bufs[1], nc)
                a1, a2 = pv(lat_v(nl - 1), bufs[0], tk, a1, a2)
                a1, a2 = pv(ctx_v(), bufs[1], nc, a1, a2)
        acc1_ref[...] = a1
        acc2_ref[...] = a2

    @pl.when(jnp.logical_not(safe))
    def _():
        def one_map(qm, k, vt, m, acc):
            s = lax.dot_general(k, qm, nt_dims, preferred_element_type=F32)
            m_new = jnp.maximum(m, jnp.max(s, axis=0, keepdims=True))
            alpha = jnp.exp2(m - m_new)
            p = jnp.exp2(s - m_new).astype(BF16)
            return m_new, alpha * acc + _dot(vt, p)

        def step(k, vt, carry):
            m1, a1, m2, a2 = carry
            m1, a1 = one_map(qa, k, vt, m1, a1)
            m2, a2 = one_map(qb, k, vt, m2, a2)
            return m1, a1, m2, a2

        m0 = jnp.full((1, tq), -jnp.inf, F32)
        carry = (m0, a0, m0, a0)
        if nl:
            carry = lax.fori_loop(0, nl, lambda t, c: step(lat_k(t), lat_v(t), c), carry)
        _, a1, _, a2 = step(ctx_k(), ctx_v(), carry)
        acc1_ref[...] = a1
        acc2_ref[...] = a2

    a1 = acc1_ref[...]
    a2 = acc2_ref[...]
    lam = lam_ref[0]
    o = a1[:V_DIM] / a1[V_DIM:V_DIM + 1] - lam * (a2[:V_DIM] / a2[V_DIM:V_DIM + 1])
    o = o * lax.rsqrt(jnp.mean(o * o, axis=0, keepdims=True) + EPS)
    reps = tq // LANES
    gain = jnp.concatenate([g_ref[...]] * reps, axis=1) if reps > 1 else g_ref[...]
    o_ref[0] = (o * gain * out_scale).T.astype(o_ref.dtype)


def _attention(lam, q_src, q_col0, k_col0, lat, ctx, g_b, tq, out_scale):
    B, nq, _ = q_src.shape
    pc, vtc = ctx
    nc = pc.shape[1]
    has_lat = lat is not None
    tk = ATTN_KEY_TILE
    assert nq % tq == 0
    head_q = pl.BlockSpec((1, tq, LANES), lambda b, h, i: (b, i, q_col0 + h))
    smem = pl.BlockSpec(memory_space=pltpu.SMEM)

    def kv_specs(n):
        return [pl.BlockSpec((1, n, LANES), lambda b, h, i: (b, 0, k_col0 + h)),
                pl.BlockSpec((1, 1, V_ROWS, n), lambda b, h, i: (b, h, 0, 0))]

    in_specs = [smem, head_q]
    args = [lam, q_src]
    if has_lat:
        p, vt = lat
        assert p.shape[1] % tk == 0
        in_specs += kv_specs(p.shape[1])
        args += [p, vt]
    in_specs += kv_specs(nc) + [pl.BlockSpec((V_DIM, LANES), lambda b, h, i: (0, 0))]
    args += [pc, vtc, g_b]
    prows = max(tk, nc) if has_lat else nc
    kern = functools.partial(_attn_kernel, tk=tk, has_lat=has_lat, out_scale=out_scale)
    return pl.pallas_call(
        kern,
        out_shape=jax.ShapeDtypeStruct((B, nq, HEADS * V_DIM), BF16),
        grid=(B, HEADS, nq // tq),
        in_specs=in_specs,
        out_specs=pl.BlockSpec((1, tq, LANES), lambda b, h, i: (b, i, h)),
        scratch_shapes=[pltpu.VMEM((8, LANES), F32),
                        pltpu.VMEM((V_ROWS, tq), F32),
                        pltpu.VMEM((V_ROWS, tq), F32)] + [pltpu.VMEM((prows, tq), BF16)] * 4,
        compiler_params=_cparams(("arbitrary", "arbitrary", "arbitrary")),
        name="diff_attn",
    )(*args)


def _mixer_epilogue(x, y, g1, n2g, sh2, sc2, rw, x1_ref, hp_ref, lg_ref):
    x1 = x + g1 * y
    x1_ref[0] = x1
    h2 = _rms_mod(x1, n2g, sh2, sc2)
    half = h2.shape[1] // 2
    hp_ref[0] = _pack_pair(h2[:, :half], h2[:, half:])
    lg_ref[0] = _dot(h2, rw)


def _proj_out_kernel(gm_ref, o_ref, w_ref, x_ref, g1_ref, n2g_ref, sh2_ref, sc2_ref, rw_ref,
                     x1_ref, hp_ref, lg_ref):
    gw = gm_ref.shape[2]
    y = _dot(gm_ref[0], w_ref[:gw, :]) + _dot(o_ref[0], w_ref[gw:, :])
    _mixer_epilogue(x_ref[0], y, g1_ref[0], n2g_ref[...], sh2_ref[0], sc2_ref[0], rw_ref[...],
                    x1_ref, hp_ref, lg_ref)


def _epilogue_out(B, N, D, tm):
    shapes = (jax.ShapeDtypeStruct((B, N, D), F32),
              jax.ShapeDtypeStruct((B, N, D // 2), U32),
              jax.ShapeDtypeStruct((B, N, LANES), F32))
    specs = (pl.BlockSpec((1, tm, D), lambda b, i: (b, i, 0)),
             pl.BlockSpec((1, tm, D // 2), lambda b, i: (b, i, 0)),
             pl.BlockSpec((1, tm, LANES), lambda b, i: (b, i, 0)))
    return shapes, specs


def _proj_out(gm, o, w_bf, x, g1, n2g, sh2, sc2, rw_pad, tm):
    B, N, D = x.shape
    gw = gm.shape[2]
    vec = pl.BlockSpec((1, 1, D), lambda b, i: (b, 0, 0))
    shapes, specs = _epilogue_out(B, N, D, tm)
    return pl.pallas_call(
        _proj_out_kernel,
        out_shape=shapes,
        grid=(B, N // tm),
        in_specs=[pl.BlockSpec((1, tm, gw), lambda b, i: (b, i, 0)),
                  pl.BlockSpec((1, tm, o.shape[2]), lambda b, i: (b, i, 0)),
                  pl.BlockSpec(w_bf.shape, lambda b, i: (0, 0)),
                  pl.BlockSpec((1, tm, D), lambda b, i: (b, i, 0)),
                  vec,
                  pl.BlockSpec((1, D), lambda b, i: (0, 0)),
                  vec, vec,
                  pl.BlockSpec((D, LANES), lambda b, i: (0, 0))],
        out_specs=specs,
        compiler_params=_cparams(("arbitrary", "arbitrary")),
        name="proj_out",
    )(gm, o, w_bf, x, g1, n2g, sh2, sc2, rw_pad)


def _pool_kernel(x_ref, xp_ref, xn_ref, n1g_ref, sh1_ref, sc1_ref, pw_ref, ps_ref,
                 g1_ref, n2g_ref, sh2_ref, sc2_ref, rw_ref, x1_ref, hp_ref, lg_ref, y_ref, *, seq):
    i = pl.program_id(1)
    last = pl.num_programs(1) - 1
    x = x_ref[0]
    tm, D = x.shape
    n1g, sh1, sc1 = n1g_ref[...], sh1_ref[0], sc1_ref[0]
    hc = _rms_mod(x, n1g, sh1, sc1)
    hp = _rms_mod(xp_ref[0], n1g, sh1, sc1) * (i > 0).astype(F32)
    hn = _rms_mod(xn_ref[0], n1g, sh1, sc1) * (i < last).astype(F32)
    he = jnp.concatenate([hp, hc, hn], axis=0).astype(BF16)
    groups = len(POOL_WINDOWS)
    gd = D // groups
    row = lax.broadcasted_iota(I32, (tm, tm + 2 * POOL_HALO), 0)
    col = lax.broadcasted_iota(I32, (tm, tm + 2 * POOL_HALO), 1)
    tglob = i * tm + lax.broadcasted_iota(I32, (tm, 1), 0)
    for g, w in enumerate(POOL_WINDOWS):
        lo = row + (POOL_HALO - w // 2)
        band = jnp.where(jnp.logical_and(col >= lo, col < lo + w), 1.0, 0.0).astype(BF16)
        s = _dot(band, he[:, g * gd:(g + 1) * gd])
        cnt = jnp.minimum(tglob + w // 2, seq) - jnp.maximum(tglob - w // 2, 0)
        pooled = s / cnt.astype(F32) - hc[:, g * gd:(g + 1) * gd]
        y_ref[:, g * gd:(g + 1) * gd] = _dot(pooled.astype(BF16), pw_ref[g])
    y = y_ref[...] * ps_ref[...]
    _mixer_epilogue(x, y, g1_ref[0], n2g_ref[...], sh2_ref[0], sc2_ref[0], rw_ref[...],
                    x1_ref, hp_ref, lg_ref)


def _pool_mixer(x, n1g, sh1, sc1, pw_bf, ps, g1, n2g, sh2, sc2, rw_pad, tm):
    B, N, D = x.shape
    gd = D // len(POOL_WINDOWS)
    hb = tm // POOL_HALO
    nhb = N // POOL_HALO
    vec = pl.BlockSpec((1, 1, D), lambda b, i: (b, 0, 0))
    row = pl.BlockSpec((1, D), lambda b, i: (0, 0))
    shapes, specs = _epilogue_out(B, N, D, tm)
    kern = functools.partial(_pool_kernel, seq=N)
    return pl.pallas_call(
        kern,
        out_shape=shapes,
        grid=(B, N // tm),
        in_specs=[pl.BlockSpec((1, tm, D), lambda b, i: (b, i, 0)),
                  pl.BlockSpec((1, POOL_HALO, D), lambda b, i: (b, jnp.maximum(i * hb - 1, 0), 0)),
                  pl.BlockSpec((1, POOL_HALO, D), lambda b, i: (b, jnp.minimum((i + 1) * hb, nhb - 1), 0)),
                  row, vec, vec,
                  pl.BlockSpec((len(POOL_WINDOWS), gd, gd), lambda b, i: (0, 0, 0)),
                  row, vec, row, vec, vec,
                  pl.BlockSpec((D, LANES), lambda b, i: (0, 0))],
        out_specs=specs,
        scratch_shapes=[pltpu.VMEM((tm, D), F32)],
        compiler_params=_cparams(("arbitrary", "arbitrary")),
        name="pool_mixer",
    )(x, x, x, n1g, sh1, sc1, pw_bf, ps, g1, n2g, sh2, sc2, rw_pad)


def _route_kernel(lg_ref, b_ref, idx_ref, w_ref, cnt_ref, carry_ref):
    @pl.when(pl.program_id(0) == 0)
    def _():
        carry_ref[...] = jnp.zeros_like(carry_ref)

    logits = lg_ref[...]
    tr = logits.shape[0]
    scores = _sigmoid(logits)
    sel = scores + b_ref[...]
    lane = lax.broadcasted_iota(I32, (tr, LANES), 1)
    picked = jnp.zeros((tr, LANES), F32)
    idx_o = jnp.zeros((tr, LANES), I32)
    w_o = jnp.zeros((tr, LANES), F32)
    wsum = jnp.zeros((tr, 1), F32)
    for k in range(TOP_K):
        mx = jnp.max(sel, axis=1, keepdims=True)
        first = jnp.min(jnp.where(sel == mx, lane, LANES), axis=1, keepdims=True)
        oh = lane == first
        wk = jnp.sum(jnp.where(oh, scores, 0.0), axis=1, keepdims=True)
        wsum = wsum + wk
        sel = jnp.where(oh, -jnp.inf, sel)
        picked = jnp.where(oh, 1.0, picked)
        idx_o = jnp.where(lane == k, first, idx_o)
        w_o = jnp.where(lane == k, wk, w_o)
    idx_ref[...] = idx_o
    w_ref[...] = w_o / wsum * ROUTED_SCALE
    carry_ref[...] = carry_ref[...] + jnp.sum(picked, axis=0, keepdims=True)
    cnt_ref[...] = carry_ref[...]


def _route(logits, bias_pad, tr):
    T = logits.shape[0]
    blk = pl.BlockSpec((tr, LANES), lambda i: (i, 0))
    one = pl.BlockSpec((1, LANES), lambda i: (0, 0))
    return pl.pallas_call(
        _route_kernel,
        out_shape=(jax.ShapeDtypeStruct((T, LANES), I32),
                   jax.ShapeDtypeStruct((T, LANES), F32),
                   jax.ShapeDtypeStruct((1, LANES), F32)),
        grid=(T // tr,),
        in_specs=[blk, one],
        out_specs=(blk, blk, one),
        scratch_shapes=[pltpu.VMEM((1, LANES), F32)],
        compiler_params=_cparams(("arbitrary",)),
        name="route",
    )(logits, bias_pad)


def _experts_kernel(be_ref, bv_ref, tok_next_ref, tok_first_ref, dst_prev_ref, hp_ref,
                    wg_ref, wu_ref, wd_ref, o8_ref, wgb, wub, wdb, xb0, xb1, yb0, yb1, gsem, ssem):
    i = pl.program_id(0)
    rows = xb0.shape[0]
    valid = bv_ref[i] != 0
    prev_valid = jnp.logical_and(i > 0, bv_ref[jnp.maximum(i - 1, 0)] != 0)
    fresh = jnp.logical_or(i == 0, be_ref[i] != be_ref[jnp.maximum(i - 1, 0)])
    xbufs, ybufs = (xb0, xb1), (yb0, yb1)

    def gather(tok_ref, slot):
        return [pltpu.make_async_copy(hp_ref.at[pl.ds(tok_ref[0, 0, r], 1), :],
                                      xbufs[slot].at[pl.ds(r, 1), :], gsem.at[slot]) for r in range(rows)]

    def gather_wait(slot):
        for r in range(rows):
            pltpu.make_async_copy(hp_ref.at[pl.ds(0, 1), :], xbufs[slot].at[pl.ds(r, 1), :],
                                  gsem.at[slot]).wait()

    def scatter(slot):
        return [pltpu.make_async_copy(ybufs[slot].at[pl.ds(r, 1), :],
                                      o8_ref.at[pl.ds(dst_prev_ref[0, 0, r], 1), :], ssem) for r in range(rows)]

    def start_all(copies, priority):
        for cp in copies:
            cp.start(priority=priority)

    def compute(slot):
        lo, hi = _unpack_pair(xbufs[slot][...])
        lo, hi = lo.astype(BF16), hi.astype(BF16)
        half = lo.shape[1]
        g = _dot(lo, wgb[:half, :]) + _dot(hi, wgb[half:, :])
        u = _dot(lo, wub[:half, :]) + _dot(hi, wub[half:, :])
        a = (g * _sigmoid(g) * u).astype(BF16)
        y = _dot(a, wdb[...])
        ybufs[slot][...] = _pack_pair(y[:, :half], y[:, half:])

    @pl.when(jnp.logical_and(valid, fresh))
    def _():
        wgb[...] = wg_ref[0, 0].astype(BF16)
        wub[...] = wu_ref[0, 0].astype(BF16)
        wdb[...] = wd_ref[0, 0].astype(BF16)

    @pl.when(jnp.logical_and(valid, i == 0))
    def _():
        start_all(gather(tok_first_ref, 0), GATHER_PRIORITY)
        gather_wait(0)
        start_all(gather(tok_next_ref, 1), GATHER_PRIORITY)
        compute(0)
        gather_wait(1)

    for par in (0, 1):
        @pl.when(jnp.logical_and(jnp.logical_and(valid, i > 0), i % 2 == par))
        def _():
            start_all(gather(tok_next_ref, 1 - par), GATHER_PRIORITY)
            out_copies = scatter(1 - par)
            start_all(out_copies, SCATTER_PRIORITY)
            compute(par)
            for cp in out_copies:
                cp.wait()
            gather_wait(1 - par)

        @pl.when(jnp.logical_and(jnp.logical_and(jnp.logical_not(valid), prev_valid), i % 2 == par))
        def _():
            out_copies = scatter(1 - par)
            start_all(out_copies, SCATTER_PRIORITY)
            for cp in out_copies:
                cp.wait()


def _experts(block_e, block_valid, tok3, dst3, hp, layer, wg, wu, wd, out_rows):
    T, W = hp.shape
    nb, _, R = tok3.shape
    _, _, D, ED = wg.shape
    smem_blk = lambda f: pl.BlockSpec((1, 1, R), f, memory_space=pltpu.SMEM)
    grid_spec = pltpu.PrefetchScalarGridSpec(
        num_scalar_prefetch=2,
        grid=(nb,),
        in_specs=[smem_blk(lambda i, be, bv: (jnp.minimum(i + 1, nb - 1), 0, 0)),
                  smem_blk(lambda i, be, bv: (0, 0, 0)),
                  smem_blk(lambda i, be, bv: (jnp.maximum(i - 1, 0), 0, 0)),
                  pl.BlockSpec(memory_space=pl.ANY),
                  pl.BlockSpec((1, 1, D, ED), lambda i, be, bv: (layer, be[i], 0, 0)),
                  pl.BlockSpec((1, 1, D, ED), lambda i, be, bv: (layer, be[i], 0, 0)),
                  pl.BlockSpec((1, 1, ED, D), lambda i, be, bv: (layer, be[i], 0, 0))],
        out_specs=pl.BlockSpec(memory_space=pl.ANY),
        scratch_shapes=[pltpu.VMEM((D, ED), BF16), pltpu.VMEM((D, ED), BF16), pltpu.VMEM((ED, D), BF16),
                        pltpu.VMEM((R, W), U32), pltpu.VMEM((R, W), U32),
                        pltpu.VMEM((R, W), U32), pltpu.VMEM((R, W), U32),
                        pltpu.SemaphoreType.DMA((2,)), pltpu.SemaphoreType.DMA(())],
    )
    return pl.pallas_call(
        _experts_kernel,
        out_shape=jax.ShapeDtypeStruct((out_rows, W), U32),
        grid_spec=grid_spec,
        compiler_params=_cparams(("arbitrary",)),
        name="experts",
    )(block_e, block_valid, tok3, tok3, dst3, hp, wg, wu, wd)


def _combine_kernel(*refs, final_norm):
    ys = refs[:TOP_K]
    wts_ref, hp_ref, x_ref, g2_ref, sg_ref, su_ref, sd_ref, fg_ref, o_ref = refs[TOP_K:]
    lo, hi = _unpack_pair(hp_ref[...])
    lo, hi = lo.astype(BF16), hi.astype(BF16)
    half = lo.shape[1]
    g = _dot(lo, sg_ref[:half, :]) + _dot(hi, sg_ref[half:, :])
    u = _dot(lo, su_ref[:half, :]) + _dot(hi, su_ref[half:, :])
    shared = _dot((g * _sigmoid(g) * u).astype(BF16), sd_ref[...])
    wts = wts_ref[...]
    acc_lo = shared[:, :half]
    acc_hi = shared[:, half:]
    for k in range(TOP_K):
        ylo, yhi = _unpack_pair(ys[k][...])
        wk = wts[:, k:k + 1]
        acc_lo = acc_lo + wk * ylo
        acc_hi = acc_hi + wk * yhi
    g2 = g2_ref[0]
    x = x_ref[...]
    out_lo = x[:, :half] + g2[:, :half] * acc_lo
    out_hi = x[:, half:] + g2[:, half:] * acc_hi
    if final_norm:
        ms = (jnp.sum(out_lo * out_lo, axis=-1, keepdims=True)
              + jnp.sum(out_hi * out_hi, axis=-1, keepdims=True)) / (2 * half)
        inv = lax.rsqrt(ms + EPS)
        fg = fg_ref[...]
        out_lo = out_lo * inv * fg[:, :half]
        out_hi = out_hi * inv * fg[:, half:]
    o_ref[:, :half] = out_lo
    o_ref[:, half:] = out_hi


def _combine(o8, wts, hp, x1, g2, sg_bf, su_bf, sd_bf, fg, tm, tiles_per_batch, final_norm):
    T, D = x1.shape
    W = D // 2
    ED = sg_bf.shape[1]
    nt = T // tm
    kern = functools.partial(_combine_kernel, final_norm=final_norm)
    slot_specs = [pl.BlockSpec((tm, W), functools.partial(lambda i, k: (k * nt + i, 0), k=k)) for k in range(TOP_K)]
    return pl.pallas_call(
        kern,
        out_shape=jax.ShapeDtypeStruct((T, D), F32),
        grid=(nt,),
        in_specs=slot_specs + [
            pl.BlockSpec((tm, LANES), lambda i: (i, 0)),
            pl.BlockSpec((tm, W), lambda i: (i, 0)),
            pl.BlockSpec((tm, D), lambda i: (i, 0)),
            pl.BlockSpec((1, 1, D), lambda i: (i // tiles_per_batch, 0, 0)),
            pl.BlockSpec((D, ED), lambda i: (0, 0)),
            pl.BlockSpec((D, ED), lambda i: (0, 0)),
            pl.BlockSpec((ED, D), lambda i: (0, 0)),
            pl.BlockSpec((1, D), lambda i: (0, 0))],
        out_specs=pl.BlockSpec((tm, D), lambda i: (i, 0)),
        compiler_params=_cparams(("arbitrary",)),
        name="combine",
    )(*([o8] * TOP_K), wts, hp, x1, g2, sg_bf, su_bf, sd_bf, fg)


def _moe(hp, logits, x1, g2, n_out, tiles_per_batch, tm, router_b, layer, wg, wu, wd, sg, su, sd, fg,
         final_norm):
    T = hp.shape[0]
    R = EXPERT_ROWS
    E = N_EXPERTS
    A = T * TOP_K
    bias_pad = jnp.full((1, LANES), -1e30, F32).at[0, :E].set(router_b.astype(F32))
    idx, wts, counts = _route(logits, bias_pad, 256)
    counts = counts[0, :E].astype(I32)
    padded = (counts + R - 1) // R * R
    pend = jnp.cumsum(padded)
    n_blocks = -(-A // R) + E + 1
    total = n_blocks * R
    slot = jnp.arange(R, dtype=I32)[None, :]
    pad_keys = jnp.where(slot < (padded - counts)[:, None], jnp.arange(E, dtype=I32)[:, None], E)
    keys = jnp.concatenate([idx[:, :TOP_K].reshape(-1), pad_keys.reshape(-1),
                            jnp.full((total - A - E * R,), E, I32)])
    vals = jnp.concatenate([jnp.arange(A, dtype=I32), jnp.full((total - A,), -1, I32)])
    _, a_r = lax.sort((keys, vals), num_keys=1, is_stable=True)
    bstart = jnp.arange(n_blocks, dtype=I32) * R
    block_e = jnp.minimum(jnp.sum((pend[None, :] <= bstart[:, None]).astype(I32), axis=1), E - 1)
    block_valid = (bstart < pend[-1]).astype(I32)
    valid_r = a_r >= 0
    t_r = a_r // TOP_K
    tok_r = jnp.where(valid_r, t_r, 0)
    spare = TOP_K * n_out + jnp.arange(total, dtype=I32) % R
    dst_r = jnp.where(jnp.logical_and(valid_r, t_r < n_out), (a_r % TOP_K) * n_out + t_r, spare)
    o8 = _experts(block_e, block_valid, tok_r.reshape(n_blocks, 1, R), dst_r.reshape(n_blocks, 1, R),
                  hp, layer, wg, wu, wd, TOP_K * n_out + R)
    return _combine(o8, wts, hp, x1, g2, sg.astype(BF16), su.astype(BF16), sd.astype(BF16),
                    fg, tm, tiles_per_batch, final_norm)


def _rope_tables(n):
    rows = n // GRID_W
    row = jnp.repeat(jnp.arange(rows, dtype=F32), GRID_W)
    col = jnp.tile(jnp.arange(GRID_W, dtype=F32), rows)
    half = QK_DIM // 2
    inv = 1.0 / (ROPE_THETA ** (jnp.arange(0, half, 2, dtype=F32) / half))
    ang_r = row[:, None] * inv[None, :]
    ang_c = col[:, None] * inv[None, :]
    cr, sr, cc, sn = jnp.cos(ang_r), jnp.sin(ang_r), jnp.cos(ang_c), jnp.sin(ang_c)
    cos64 = jnp.concatenate([cr, cr, cc, cc], axis=1)
    sin64 = jnp.concatenate([-sr, sr, -sn, sn], axis=1)
    reps = LANES // QK_DIM
    return jnp.tile(cos64, (1, reps)), jnp.tile(sin64, (1, reps))


def kernel(x, c, ctx, c_ctx, ada_w, ada_b, norm1_g, norm2_g, w_in, w_out, gmlp_norm_g, gmlp_ws, gmlp_bs,
           lam_q1, lam_k1, lam_q2, lam_k2, subln_g, pool_w, pool_scale, router_w, router_b, exp_gate,
           exp_up, exp_down, sh_gate, sh_up, sh_down, final_g):
    B, N, D = x.shape
    NC = ctx.shape[1]
    depth = ada_w.shape[0]
    assert depth == 2 and B + 1 <= 8

    s8 = jnp.zeros((8, D), F32).at[:B].set(jax.nn.silu(c)).at[B].set(jax.nn.silu(c_ctx))
    mod = _ada_mod(s8, ada_w, ada_b).reshape(depth, 8, 6, D)

    def mods(layer, rows):
        m = mod[layer, rows]
        sh1, s1, g1, sh2, s2, g2 = (m[:, k][:, None, :] for k in range(6))
        return sh1, 1.0 + s1, g1, sh2, 1.0 + s2, g2

    rw_pad = [jnp.zeros((D, LANES), F32).at[:, :N_EXPERTS].set(router_w[l]) for l in range(depth)]
    row = lambda v: v.reshape(1, -1).astype(F32)

    sh1, sc1, g1, sh2, sc2, g2 = mods(0, slice(0, B))
    csh1, csc1, cg1, csh2, csc2, _ = (jnp.broadcast_to(v, (B, 1, D)) for v in mods(0, slice(B, B + 1)))
    lam_init = 0.8 - 0.6 * math.exp(-0.3 * 0)
    lam = (jnp.exp(jnp.sum(lam_q1[0] * lam_k1[0])) - jnp.exp(jnp.sum(lam_q2[0] * lam_k2[0]))
           + lam_init).reshape(1).astype(F32)
    w_in_bf = w_in[0].astype(BF16)
    cos_t, sin_t = _rope_tables(N)
    qscale = QK_DIM ** -0.5 * math.log2(math.e)
    tabs = (cos_t * qscale, sin_t * qscale, cos_t, sin_t)
    ones_c = jnp.ones((NC, LANES), F32)
    zeros_c = jnp.zeros((NC, LANES), F32)
    tabs_c = (ones_c * qscale, zeros_c, ones_c, zeros_c)

    p, vt = _proj_in(x, row(norm1_g[0]), sh1, sc1, w_in_bf, tabs, min(512, N))
    pc, vtc = _proj_in(ctx, row(norm1_g[0]), csh1, csc1, w_in_bf, tabs_c, min(512, NC))

    gw = gmlp_norm_g.shape[1]
    gd = gw // GMLP_GROUPS
    ws_bf = gmlp_ws[0].astype(BF16)
    bs_b = jnp.broadcast_to(gmlp_bs[0][:, :, None], (GMLP_GROUPS, CHUNK, gd)).astype(F32)
    gm = _gmlp(p, row(gmlp_norm_g[0]), ws_bf, bs_b, 256)
    gmc = _gmlp(pc, row(gmlp_norm_g[0]), ws_bf, bs_b, 256)

    g_b = jnp.broadcast_to(subln_g[0].astype(F32)[:, None], (V_DIM, LANES))
    q_col0 = 2 * gw // LANES
    k_col0 = q_col0 + HEADS * 2 * QK_DIM // LANES
    out_scale = 1.0 - lam_init
    o = _attention(lam, p, q_col0, k_col0, (p, vt), (pc, vtc), g_b, min(1024, N), out_scale)
    oc = _attention(lam, pc, q_col0, k_col0, None, (pc, vtc), g_b, NC, out_scale)

    w_out_bf = w_out[0].astype(BF16)
    x1, hp, lg = _proj_out(gm, o, w_out_bf, x, g1, row(norm2_g[0]), sh2, sc2, rw_pad[0], 256)
    _, hpc, lgc = _proj_out(gmc, oc, w_out_bf, ctx, cg1, row(norm2_g[0]), csh2, csc2, rw_pad[0], 256)

    T_lat = B * N
    hp_all = jnp.concatenate([hp.reshape(T_lat, -1), hpc.reshape(B * NC, -1)], axis=0)
    lg_all = jnp.concatenate([lg.reshape(T_lat, LANES), lgc.reshape(B * NC, LANES)], axis=0)
    tm_c = 128
    x = _moe(hp_all, lg_all, x1.reshape(T_lat, D), g2, T_lat, N // tm_c, tm_c, router_b[0],
             0, exp_gate, exp_up, exp_down, sh_gate[0], sh_up[0], sh_down[0],
             row(final_g), False).reshape(B, N, D)

    sh1, sc1, g1, sh2, sc2, g2 = mods(1, slice(0, B))
    x1, hp, lg = _pool_mixer(x, row(norm1_g[1]), sh1, sc1, pool_w[0].astype(BF16), row(pool_scale[0]),
                             g1, row(norm2_g[1]), sh2, sc2, rw_pad[1], 256)
    out = _moe(hp.reshape(T_lat, -1), lg.reshape(T_lat, LANES), x1.reshape(T_lat, D), g2, T_lat,
               N // tm_c, tm_c, router_b[1], 1, exp_gate, exp_up, exp_down,
               sh_gate[1], sh_up[1], sh_down[1], row(final_g), True)
    return out.reshape(B, N, D)
```

```python
import functools
import math

import jax
import jax.numpy as jnp
from jax import lax
from jax.experimental import pallas as pl
from jax.experimental.pallas import tpu as pltpu

EPS = 1e-6
GRID_W = 64
ROPE_THETA = 10000.0
HEADS = 8
QK_DIM = 64
V_DIM = 128
V_ROWS = V_DIM + 16
GMLP_GROUPS = 4
CHUNK = 128
POOL_WINDOWS = (2, 4, 8, 16)
POOL_HALO = 8
N_EXPERTS = 64
TOP_K = 8
ROUTED_SCALE = 2.5
LANES = 128
EXPERT_ROWS = 256
ATTN_KEY_TILE = 1024
ROW_TILE = 8
GATHER_PRIORITY = 0
SCATTER_PRIORITY = 1
SAFE_SCORE_BOUND = 30.0
VMEM_LIMIT = 56 * 1024 * 1024

BF16 = jnp.bfloat16
F32 = jnp.float32
U32 = jnp.uint32
I32 = jnp.int32


def _cparams(sem):
    return pltpu.CompilerParams(dimension_semantics=sem, vmem_limit_bytes=VMEM_LIMIT)


def _dot(a, b):
    return jnp.dot(a, b, preferred_element_type=F32)


def _pack_pair(lo, hi):
    ulo = pltpu.bitcast(lo.astype(BF16).astype(F32), U32)
    uhi = pltpu.bitcast(hi.astype(BF16).astype(F32), U32)
    return (ulo >> 16) | uhi


def _unpack_pair(u):
    lo = pltpu.bitcast(u << 16, F32)
    hi = pltpu.bitcast(u & jnp.uint32(0xFFFF0000), F32)
    return lo, hi


def _store_rows(ref, rows):
    n = rows.shape[0]
    for j in range(ROW_TILE):
        ref[pl.ds(j, n, stride=ROW_TILE), :] = rows[:, j * LANES:(j + 1) * LANES]


def _load_rows(ref):
    n = ref.shape[0] // ROW_TILE
    return jnp.concatenate([ref[pl.ds(j, n, stride=ROW_TILE), :] for j in range(ROW_TILE)], axis=1)


def _rms_mod(x, g, shift, scale1p):
    y = x * lax.rsqrt(jnp.mean(x * x, axis=-1, keepdims=True) + EPS)
    return y * g * scale1p + shift


def _sigmoid(x):
    return 1.0 / (1.0 + jnp.exp(-x))


def _ada_kernel(s_ref, w_ref, b_ref, o_ref):
    o_ref[0] = _dot(s_ref[...].astype(BF16), w_ref[0].astype(BF16)) + b_ref[0]


def _ada_mod(s8, ada_w, ada_b):
    L, D, W = ada_w.shape
    tn = 768
    return pl.pallas_call(
        _ada_kernel,
        out_shape=jax.ShapeDtypeStruct((L, 8, W), F32),
        grid=(L, W // tn),
        in_specs=[pl.BlockSpec((8, D), lambda l, j: (0, 0)),
                  pl.BlockSpec((1, D, tn), lambda l, j: (l, 0, j)),
                  pl.BlockSpec((1, 1, tn), lambda l, j: (l, 0, j))],
        out_specs=pl.BlockSpec((1, 8, tn), lambda l, j: (l, 0, j)),
        compiler_params=_cparams(("arbitrary", "arbitrary")),
        name="ada_mod",
    )(s8, ada_w, ada_b.reshape(L, 1, W))


def _proj_in_kernel(x_ref, g_ref, sh_ref, sc_ref, w_ref, cq_ref, sq_ref, ck_ref, sk_ref,
                    o_ref, vt_ref, h_ref, *, q_tile, k_tile, v_tile):
    j = pl.program_id(2)

    @pl.when(j == 0)
    def _():
        h_ref[...] = _rms_mod(x_ref[0], g_ref[...], sh_ref[0], sc_ref[0]).astype(BF16)

    tm, tn = h_ref.shape[0], w_ref.shape[1]

    def project():
        return _dot(h_ref[...], w_ref[...])

    def rope(cos_ref, sin_ref):
        acc = project()
        reps = tn // LANES
        cos = jnp.concatenate([cos_ref[...]] * reps, axis=1)
        sin = jnp.concatenate([sin_ref[...]] * reps, axis=1)
        lane = lax.broadcasted_iota(I32, acc.shape, 1)
        partner = jnp.where((lane & 16) == 0,
                            pltpu.roll(acc, tn - 16, axis=1),
                            pltpu.roll(acc, 16, axis=1))
        return acc * cos + partner * sin

    is_q = j == q_tile
    is_k = j == k_tile
    is_v = j == v_tile

    @pl.when(is_q)
    def _():
        o_ref[0] = rope(cq_ref, sq_ref).astype(o_ref.dtype)

    @pl.when(is_k)
    def _():
        o_ref[0] = rope(ck_ref, sk_ref).astype(o_ref.dtype)

    @pl.when(jnp.logical_not(jnp.logical_or(jnp.logical_or(is_q, is_k), is_v)))
    def _():
        o_ref[0] = project().astype(o_ref.dtype)

    @pl.when(is_v)
    def _():
        acc = project()
        o_ref[0] = acc.astype(o_ref.dtype)
        ones = jnp.ones((V_ROWS - V_DIM, tm), vt_ref.dtype)
        for h in range(HEADS):
            vt_ref[0, h, 0:V_DIM, :] = acc[:, h * V_DIM:(h + 1) * V_DIM].T.astype(vt_ref.dtype)
            vt_ref[0, h, V_DIM:V_ROWS, :] = ones


def _proj_in(x, g, shift, scale1p, w_bf, tabs, tm):
    B, N, D = x.shape
    W = w_bf.shape[1]
    tn = 1024
    q_tile, k_tile, v_tile = 2, 3, 4
    assert W == 5 * tn and HEADS * V_DIM == tn
    kern = functools.partial(_proj_in_kernel, q_tile=q_tile, k_tile=k_tile, v_tile=v_tile)
    vec = pl.BlockSpec((1, 1, D), lambda b, i, j: (b, 0, 0))
    tab = pl.BlockSpec((tm, LANES), lambda b, i, j: (i, 0))
    return pl.pallas_call(
        kern,
        out_shape=(jax.ShapeDtypeStruct((B, N, W), BF16),
                   jax.ShapeDtypeStruct((B, HEADS, V_ROWS, N), BF16)),
        grid=(B, N // tm, W // tn),
        in_specs=[pl.BlockSpec((1, tm, D), lambda b, i, j: (b, i, 0)),
                  pl.BlockSpec((1, D), lambda b, i, j: (0, 0)),
                  vec, vec,
                  pl.BlockSpec((D, tn), lambda b, i, j: (0, j)),
                  tab, tab, tab, tab],
        out_specs=(pl.BlockSpec((1, tm, tn), lambda b, i, j: (b, i, j)),
                   pl.BlockSpec((1, HEADS, V_ROWS, tm), lambda b, i, j: (b, 0, 0, i))),
        scratch_shapes=[pltpu.VMEM((tm, D), BF16)],
        compiler_params=_cparams(("arbitrary", "arbitrary", "arbitrary")),
        name="proj_in",
    )(x, g, shift, scale1p, w_bf, *tabs)


def _gelu_tanh(x):
    return 0.5 * x * (1.0 + jnp.tanh(math.sqrt(2.0 / math.pi) * (x + 0.044715 * (x * x * x))))


def _gmlp_kernel(uv_ref, ng_ref, ws_ref, bs_ref, o_ref, *, chunks):
    gw = o_ref.shape[2]
    gd = gw // GMLP_GROUPS
    for c in range(chunks):
        rows = pl.ds(c * CHUNK, CHUNK)
        for g in range(GMLP_GROUPS):
            u = _gelu_tanh(uv_ref[0, rows, g * gd:(g + 1) * gd].astype(F32))
            v = _gelu_tanh(uv_ref[0, rows, gw + g * gd:gw + (g + 1) * gd].astype(F32))
            vn = v * lax.rsqrt(jnp.mean(v * v, axis=-1, keepdims=True) + EPS) * ng_ref[:, g * gd:(g + 1) * gd]
            mixed = _dot(ws_ref[g], vn.astype(BF16)) + bs_ref[g]
            o_ref[0, rows, g * gd:(g + 1) * gd] = (u * mixed).astype(o_ref.dtype)


def _gmlp(p, norm_g, ws_bf, bs_b, tg):
    B, N, _ = p.shape
    gw = norm_g.shape[1]
    gd = gw // GMLP_GROUPS
    kern = functools.partial(_gmlp_kernel, chunks=tg // CHUNK)
    return pl.pallas_call(
        kern,
        out_shape=jax.ShapeDtypeStruct((B, N, gw), BF16),
        grid=(B, N // tg),
        in_specs=[pl.BlockSpec((1, tg, 2 * gw), lambda b, i: (b, i, 0)),
                  pl.BlockSpec((1, gw), lambda b, i: (0, 0)),
                  pl.BlockSpec((GMLP_GROUPS, CHUNK, CHUNK), lambda b, i: (0, 0, 0)),
                  pl.BlockSpec((GMLP_GROUPS, CHUNK, gd), lambda b, i: (0, 0, 0))],
        out_specs=pl.BlockSpec((1, tg, gw), lambda b, i: (b, i, 0)),
        compiler_params=_cparams(("arbitrary", "arbitrary")),
        name="gmlp",
    )(p, norm_g, ws_bf, bs_b)


def _attn_kernel(*refs, tk, has_lat, out_scale):
    if has_lat:
        (lam_ref, q_ref, kl_ref, vl_ref, kc_ref, vc_ref, g_ref, o_ref,
         kmax_ref, acc1_ref, acc2_ref, pa1_ref, pa2_ref, pb1_ref, pb2_ref) = refs
        nl = kl_ref.shape[1] // tk
    else:
        (lam_ref, q_ref, kc_ref, vc_ref, g_ref, o_ref,
         kmax_ref, acc1_ref, acc2_ref, pa1_ref, pa2_ref, pb1_ref, pb2_ref) = refs
        nl = 0
    i = pl.program_id(2)
    tq = q_ref.shape[1]
    nc = kc_ref.shape[1]
    nt_dims = (((1,), (1,)), ((), ()))
    sub = lax.broadcasted_iota(I32, (8, LANES), 0)
    ln8 = lax.broadcasted_iota(I32, (8, LANES), 1)
    half_sel = jnp.where(jnp.logical_or(jnp.logical_and(sub == 0, ln8 < QK_DIM),
                                        jnp.logical_and(sub == 1, ln8 >= QK_DIM)), 1.0, 0.0).astype(BF16)

    def sq_norms(rows):
        rf = rows.astype(F32)
        return lax.dot_general(half_sel, (rf * rf).astype(BF16), nt_dims, preferred_element_type=F32)

    def lat_start(t):
        return t * tk if isinstance(t, int) else pl.multiple_of(t * tk, tk)

    def lat_k(t):
        return kl_ref[0, pl.ds(lat_start(t), tk), :]

    def lat_v(t):
        return vl_ref[0, 0, :, pl.ds(lat_start(t), tk)]

    def ctx_k():
        return kc_ref[0]

    def ctx_v():
        return vc_ref[0, 0]

    @pl.when(i == 0)
    def _():
        mx = jnp.max(sq_norms(ctx_k()), axis=1, keepdims=True)
        if nl:
            mx = lax.fori_loop(
                0, nl, lambda t, m: jnp.maximum(m, jnp.max(sq_norms(lat_k(t)), axis=1, keepdims=True)), mx)
        kmax_ref[...] = jnp.broadcast_to(mx, (8, LANES))

    q = q_ref[0]
    lane = lax.broadcasted_iota(I32, q.shape, 1)
    zero = jnp.zeros_like(q)
    qa = jnp.where(lane < QK_DIM, q, zero)
    qb = jnp.where(lane >= QK_DIM, q, zero)
    bound = 1.01 * jnp.sqrt(sq_norms(q) * kmax_ref[:, 0:1]) + 1e-3
    b1, b2 = bound[0:1], bound[1:2]
    safe = jnp.max(bound[0:2]) <= SAFE_SCORE_BOUND
    a0 = jnp.zeros((V_ROWS, tq), F32)

    @pl.when(safe)
    def _():
        bufs = ((pa1_ref, pa2_ref), (pb1_ref, pb2_ref))

        def qk(k, buf, rows):
            s1 = lax.dot_general(k, qa, nt_dims, preferred_element_type=F32)
            buf[0][0:rows, :] = jnp.exp2(s1 - b1).astype(BF16)
            s2 = lax.dot_general(k, qb, nt_dims, preferred_element_type=F32)
            buf[1][0:rows, :] = jnp.exp2(s2 - b2).astype(BF16)

        def pv(vt, buf, rows, a1, a2):
            a1 = a1 + _dot(vt, buf[0][0:rows, :])
            a2 = a2 + _dot(vt, buf[1][0:rows, :])
            return a1, a2

        if nl == 0:
            qk(ctx_k(), bufs[0], nc)
            a1, a2 = pv(ctx_v(), bufs[0], nc, a0, a0)
        else:
            qk(lat_k(0), bufs[0], tk)

            def pair(j, carry):
                a1, a2 = carry
                t = 2 * j
                qk(lat_k(t + 1), bufs[1], tk)
                a1, a2 = pv(lat_v(t), bufs[0], tk, a1, a2)
                qk(lat_k(t + 2), bufs[0], tk)
                return pv(lat_v(t + 1), bufs[1], tk, a1, a2)

            a1, a2 = lax.fori_loop(0, (nl - 1) // 2, pair, (a0, a0))
            if (nl - 1) % 2 == 1:
                qk(lat_k(nl - 1), bufs[1], tk)
                a1, a2 = pv(lat_v(nl - 2), bufs[0], tk, a1, a2)
                qk(ctx_k(), bufs[0], nc)
                a1, a2 = pv(lat_v(nl - 1), bufs[1], tk, a1, a2)
                a1, a2 = pv(ctx_v(), bufs[0], nc, a1, a2)
            else:
                qk(ctx_k(), bufs[1], nc)
                a1, a2 = pv(lat_v(nl - 1), bufs[0], tk, a1, a2)
                a1, a2 = pv(ctx_v(), bufs[1], nc, a1, a2)
        acc1_ref[...] = a1
        acc2_ref[...] = a2

    @pl.when(jnp.logical_not(safe))
    def _():
        def one_map(qm, k, vt, m, acc):
            s = lax.dot_general(k, qm, nt_dims, preferred_element_type=F32)
            m_new = jnp.maximum(m, jnp.max(s, axis=0, keepdims=True))
            alpha = jnp.exp2(m - m_new)
            p = jnp.exp2(s - m_new).astype(BF16)
            return m_new, alpha * acc + _dot(vt, p)

        def step(k, vt, carry):
            m1, a1, m2, a2 = carry
            m1, a1 = one_map(qa, k, vt, m1, a1)
            m2, a2 = one_map(qb, k, vt, m2, a2)
            return m1, a1, m2, a2

        m0 = jnp.full((1, tq), -jnp.inf, F32)
        carry = (m0, a0, m0, a0)
        if nl:
            carry = lax.fori_loop(0, nl, lambda t, c: step(lat_k(t), lat_v(t), c), carry)
        _, a1, _, a2 = step(ctx_k(), ctx_v(), carry)
        acc1_ref[...] = a1
        acc2_ref[...] = a2

    a1 = acc1_ref[...]
    a2 = acc2_ref[...]
    lam = lam_ref[0]
    o = a1[:V_DIM] / a1[V_DIM:V_DIM + 1] - lam * (a2[:V_DIM] / a2[V_DIM:V_DIM + 1])
    o = o * lax.rsqrt(jnp.mean(o * o, axis=0, keepdims=True) + EPS)
    reps = tq // LANES
    gain = jnp.concatenate([g_ref[...]] * reps, axis=1) if reps > 1 else g_ref[...]
    o_ref[0] = (o * gain * out_scale).T.astype(o_ref.dtype)


def _attention(lam, q_src, q_col0, k_col0, lat, ctx, g_b, tq, out_scale):
    B, nq, _ = q_src.shape
    pc, vtc = ctx
    nc = pc.shape[1]
    has_lat = lat is not None
    tk = ATTN_KEY_TILE
    assert nq % tq == 0
    head_q = pl.BlockSpec((1, tq, LANES), lambda b, h, i: (b, i, q_col0 + h))
    smem = pl.BlockSpec(memory_space=pltpu.SMEM)

    def kv_specs(n):
        return [pl.BlockSpec((1, n, LANES), lambda b, h, i: (b, 0, k_col0 + h)),
                pl.BlockSpec((1, 1, V_ROWS, n), lambda b, h, i: (b, h, 0, 0))]

    in_specs = [smem, head_q]
    args = [lam, q_src]
    if has_lat:
        p, vt = lat
        assert p.shape[1] % tk == 0
        in_specs += kv_specs(p.shape[1])
        args += [p, vt]
    in_specs += kv_specs(nc) + [pl.BlockSpec((V_DIM, LANES), lambda b, h, i: (0, 0))]
    args += [pc, vtc, g_b]
    prows = max(tk, nc) if has_lat else nc
    kern = functools.partial(_attn_kernel, tk=tk, has_lat=has_lat, out_scale=out_scale)
    return pl.pallas_call(
        kern,
        out_shape=jax.ShapeDtypeStruct((B, nq, HEADS * V_DIM), BF16),
        grid=(B, HEADS, nq // tq),
        in_specs=in_specs,
        out_specs=pl.BlockSpec((1, tq, LANES), lambda b, h, i: (b, i, h)),
        scratch_shapes=[pltpu.VMEM((8, LANES), F32),
                        pltpu.VMEM((V_ROWS, tq), F32),
                        pltpu.VMEM((V_ROWS, tq), F32)] + [pltpu.VMEM((prows, tq), BF16)] * 4,
        compiler_params=_cparams(("arbitrary", "arbitrary", "arbitrary")),
        name="diff_attn",
    )(*args)


def _mixer_epilogue(x, y, g1, n2g, sh2, sc2, rw, x1_ref, hp_ref, lg_ref):
    x1 = x + g1 * y
    x1_ref[0] = x1
    h2 = _rms_mod(x1, n2g, sh2, sc2)
    half = h2.shape[1] // 2
    _store_rows(hp_ref.at[0], _pack_pair(h2[:, :half], h2[:, half:]))
    lg_ref[0] = _dot(h2, rw)


def _proj_out_kernel(gm_ref, o_ref, w_ref, x_ref, g1_ref, n2g_ref, sh2_ref, sc2_ref, rw_ref,
                     x1_ref, hp_ref, lg_ref):
    gw = gm_ref.shape[2]
    y = _dot(gm_ref[0], w_ref[:gw, :]) + _dot(o_ref[0], w_ref[gw:, :])
    _mixer_epilogue(x_ref[0], y, g1_ref[0], n2g_ref[...], sh2_ref[0], sc2_ref[0], rw_ref[...],
                    x1_ref, hp_ref, lg_ref)


def _epilogue_out(B, N, D, tm):
    shapes = (jax.ShapeDtypeStruct((B, N, D), F32),
              jax.ShapeDtypeStruct((B, N * ROW_TILE, LANES), U32),
              jax.ShapeDtypeStruct((B, N, LANES), F32))
    specs = (pl.BlockSpec((1, tm, D), lambda b, i: (b, i, 0)),
             pl.BlockSpec((1, tm * ROW_TILE, LANES), lambda b, i: (b, i, 0)),
             pl.BlockSpec((1, tm, LANES), lambda b, i: (b, i, 0)))
    return shapes, specs


def _proj_out(gm, o, w_bf, x, g1, n2g, sh2, sc2, rw_pad, tm):
    B, N, D = x.shape
    gw = gm.shape[2]
    vec = pl.BlockSpec((1, 1, D), lambda b, i: (b, 0, 0))
    shapes, specs = _epilogue_out(B, N, D, tm)
    return pl.pallas_call(
        _proj_out_kernel,
        out_shape=shapes,
        grid=(B, N // tm),
        in_specs=[pl.BlockSpec((1, tm, gw), lambda b, i: (b, i, 0)),
                  pl.BlockSpec((1, tm, o.shape[2]), lambda b, i: (b, i, 0)),
                  pl.BlockSpec(w_bf.shape, lambda b, i: (0, 0)),
                  pl.BlockSpec((1, tm, D), lambda b, i: (b, i, 0)),
                  vec,
                  pl.BlockSpec((1, D), lambda b, i: (0, 0)),
                  vec, vec,
                  pl.BlockSpec((D, LANES), lambda b, i: (0, 0))],
        out_specs=specs,
        compiler_params=_cparams(("arbitrary", "arbitrary")),
        name="proj_out",
    )(gm, o, w_bf, x, g1, n2g, sh2, sc2, rw_pad)


def _pool_kernel(x_ref, xp_ref, xn_ref, n1g_ref, sh1_ref, sc1_ref, pw_ref, ps_ref,
                 g1_ref, n2g_ref, sh2_ref, sc2_ref, rw_ref, x1_ref, hp_ref, lg_ref, y_ref, *, seq):
    i = pl.program_id(1)
    last = pl.num_programs(1) - 1
    x = x_ref[0]
    tm, D = x.shape
    n1g, sh1, sc1 = n1g_ref[...], sh1_ref[0], sc1_ref[0]
    hc = _rms_mod(x, n1g, sh1, sc1)
    hp = _rms_mod(xp_ref[0], n1g, sh1, sc1) * (i > 0).astype(F32)
    hn = _rms_mod(xn_ref[0], n1g, sh1, sc1) * (i < last).astype(F32)
    he = jnp.concatenate([hp, hc, hn], axis=0).astype(BF16)
    groups = len(POOL_WINDOWS)
    gd = D // groups
    row = lax.broadcasted_iota(I32, (tm, tm + 2 * POOL_HALO), 0)
    col = lax.broadcasted_iota(I32, (tm, tm + 2 * POOL_HALO), 1)
    tglob = i * tm + lax.broadcasted_iota(I32, (tm, 1), 0)
    for g, w in enumerate(POOL_WINDOWS):
        lo = row + (POOL_HALO - w // 2)
        band = jnp.where(jnp.logical_and(col >= lo, col < lo + w), 1.0, 0.0).astype(BF16)
        s = _dot(band, he[:, g * gd:(g + 1) * gd])
        cnt = jnp.minimum(tglob + w // 2, seq) - jnp.maximum(tglob - w // 2, 0)
        pooled = s / cnt.astype(F32) - hc[:, g * gd:(g + 1) * gd]
        y_ref[:, g * gd:(g + 1) * gd] = _dot(pooled.astype(BF16), pw_ref[g])
    y = y_ref[...] * ps_ref[...]
    _mixer_epilogue(x, y, g1_ref[0], n2g_ref[...], sh2_ref[0], sc2_ref[0], rw_ref[...],
                    x1_ref, hp_ref, lg_ref)


def _pool_mixer(x, n1g, sh1, sc1, pw_bf, ps, g1, n2g, sh2, sc2, rw_pad, tm):
    B, N, D = x.shape
    gd = D // len(POOL_WINDOWS)
    hb = tm // POOL_HALO
    nhb = N // POOL_HALO
    vec = pl.BlockSpec((1, 1, D), lambda b, i: (b, 0, 0))
    row = pl.BlockSpec((1, D), lambda b, i: (0, 0))
    shapes, specs = _epilogue_out(B, N, D, tm)
    kern = functools.partial(_pool_kernel, seq=N)
    return pl.pallas_call(
        kern,
        out_shape=shapes,
        grid=(B, N // tm),
        in_specs=[pl.BlockSpec((1, tm, D), lambda b, i: (b, i, 0)),
                  pl.BlockSpec((1, POOL_HALO, D), lambda b, i: (b, jnp.maximum(i * hb - 1, 0), 0)),
                  pl.BlockSpec((1, POOL_HALO, D), lambda b, i: (b, jnp.minimum((i + 1) * hb, nhb - 1), 0)),
                  row, vec, vec,
                  pl.BlockSpec((len(POOL_WINDOWS), gd, gd), lambda b, i: (0, 0, 0)),
                  row, vec, row, vec, vec,
                  pl.BlockSpec((D, LANES), lambda b, i: (0, 0))],
        out_specs=specs,
        scratch_shapes=[pltpu.VMEM((tm, D), F32)],
        compiler_params=_cparams(("arbitrary", "arbitrary")),
        name="pool_mixer",
    )(x, x, x, n1g, sh1, sc1, pw_bf, ps, g1, n2g, sh2, sc2, rw_pad)


def _route_kernel(lg_ref, b_ref, idx_ref, w_ref, cnt_ref, carry_ref):
    @pl.when(pl.program_id(0) == 0)
    def _():
        carry_ref[...] = jnp.zeros_like(carry_ref)

    logits = lg_ref[...]
    tr = logits.shape[0]
    scores = _sigmoid(logits)
    sel = scores + b_ref[...]
    lane = lax.broadcasted_iota(I32, (tr, LANES), 1)
    picked = jnp.zeros((tr, LANES), F32)
    idx_o = jnp.zeros((tr, LANES), I32)
    w_o = jnp.zeros((tr, LANES), F32)
    wsum = jnp.zeros((tr, 1), F32)
    for k in range(TOP_K):
        mx = jnp.max(sel, axis=1, keepdims=True)
        first = jnp.min(jnp.where(sel == mx, lane, LANES), axis=1, keepdims=True)
        oh = lane == first
        wk = jnp.sum(jnp.where(oh, scores, 0.0), axis=1, keepdims=True)
        wsum = wsum + wk
        sel = jnp.where(oh, -jnp.inf, sel)
        picked = jnp.where(oh, 1.0, picked)
        idx_o = jnp.where(lane == k, first, idx_o)
        w_o = jnp.where(lane == k, wk, w_o)
    idx_ref[...] = idx_o
    w_ref[...] = w_o / wsum * ROUTED_SCALE
    carry_ref[...] = carry_ref[...] + jnp.sum(picked, axis=0, keepdims=True)
    cnt_ref[...] = carry_ref[...]


def _route(logits, bias_pad, tr):
    T = logits.shape[0]
    blk = pl.BlockSpec((tr, LANES), lambda i: (i, 0))
    one = pl.BlockSpec((1, LANES), lambda i: (0, 0))
    return pl.pallas_call(
        _route_kernel,
        out_shape=(jax.ShapeDtypeStruct((T, LANES), I32),
                   jax.ShapeDtypeStruct((T, LANES), F32),
                   jax.ShapeDtypeStruct((1, LANES), F32)),
        grid=(T // tr,),
        in_specs=[blk, one],
        out_specs=(blk, blk, one),
        scratch_shapes=[pltpu.VMEM((1, LANES), F32)],
        compiler_params=_cparams(("arbitrary",)),
        name="route",
    )(logits, bias_pad)


def _experts_kernel(be_ref, bv_ref, tok_next_ref, tok_first_ref, dst_prev_ref, hp_ref,
                    wg_ref, wu_ref, wd_ref, o8_ref, wgb, wub, wdb, xb0, xb1, yb0, yb1, gsem, ssem):
    i = pl.program_id(0)
    rows = xb0.shape[0] // ROW_TILE
    valid = bv_ref[i] != 0
    prev_valid = jnp.logical_and(i > 0, bv_ref[jnp.maximum(i - 1, 0)] != 0)
    fresh = jnp.logical_or(i == 0, be_ref[i] != be_ref[jnp.maximum(i - 1, 0)])
    xbufs, ybufs = (xb0, xb1), (yb0, yb1)

    def tile_at(ref, first):
        return ref.at[pl.ds(first, ROW_TILE), :]

    def gather(tok_ref, slot):
        return [pltpu.make_async_copy(tile_at(hp_ref, pl.multiple_of(tok_ref[0, 0, r], ROW_TILE)),
                                      tile_at(xbufs[slot], r * ROW_TILE), gsem.at[slot]) for r in range(rows)]

    def gather_wait(slot):
        for r in range(rows):
            pltpu.make_async_copy(tile_at(hp_ref, 0), tile_at(xbufs[slot], r * ROW_TILE), gsem.at[slot]).wait()

    def scatter(slot):
        return [pltpu.make_async_copy(tile_at(ybufs[slot], r * ROW_TILE),
                                      tile_at(o8_ref, pl.multiple_of(dst_prev_ref[0, 0, r], ROW_TILE)), ssem)
                for r in range(rows)]

    def start_all(copies, priority):
        for cp in copies:
            cp.start(priority=priority)

    def compute(slot):
        lo, hi = _unpack_pair(_load_rows(xbufs[slot]))
        lo, hi = lo.astype(BF16), hi.astype(BF16)
        half = lo.shape[1]
        g = _dot(lo, wgb[:half, :]) + _dot(hi, wgb[half:, :])
        u = _dot(lo, wub[:half, :]) + _dot(hi, wub[half:, :])
        a = (g * _sigmoid(g) * u).astype(BF16)
        y = _dot(a, wdb[...])
        _store_rows(ybufs[slot], _pack_pair(y[:, :half], y[:, half:]))

    @pl.when(jnp.logical_and(valid, fresh))
    def _():
        wgb[...] = wg_ref[0, 0].astype(BF16)
        wub[...] = wu_ref[0, 0].astype(BF16)
        wdb[...] = wd_ref[0, 0].astype(BF16)

    @pl.when(jnp.logical_and(valid, i == 0))
    def _():
        start_all(gather(tok_first_ref, 0), GATHER_PRIORITY)
        gather_wait(0)
        start_all(gather(tok_next_ref, 1), GATHER_PRIORITY)
        compute(0)
        gather_wait(1)

    for par in (0, 1):
        @pl.when(jnp.logical_and(jnp.logical_and(valid, i > 0), i % 2 == par))
        def _():
            start_all(gather(tok_next_ref, 1 - par), GATHER_PRIORITY)
            out_copies = scatter(1 - par)
            start_all(out_copies, SCATTER_PRIORITY)
            compute(par)
            for cp in out_copies:
                cp.wait()
            gather_wait(1 - par)

        @pl.when(jnp.logical_and(jnp.logical_and(jnp.logical_not(valid), prev_valid), i % 2 == par))
        def _():
            out_copies = scatter(1 - par)
            start_all(out_copies, SCATTER_PRIORITY)
            for cp in out_copies:
                cp.wait()


def _experts(block_e, block_valid, tok3, dst3, hp, layer, wg, wu, wd, out_rows):
    nb, _, R = tok3.shape
    buf = pltpu.VMEM((R * ROW_TILE, LANES), U32)
    _, _, D, ED = wg.shape
    smem_blk = lambda f: pl.BlockSpec((1, 1, R), f, memory_space=pltpu.SMEM)
    grid_spec = pltpu.PrefetchScalarGridSpec(
        num_scalar_prefetch=2,
        grid=(nb,),
        in_specs=[smem_blk(lambda i, be, bv: (jnp.minimum(i + 1, nb - 1), 0, 0)),
                  smem_blk(lambda i, be, bv: (0, 0, 0)),
                  smem_blk(lambda i, be, bv: (jnp.maximum(i - 1, 0), 0, 0)),
                  pl.BlockSpec(memory_space=pl.ANY),
                  pl.BlockSpec((1, 1, D, ED), lambda i, be, bv: (layer, be[i], 0, 0)),
                  pl.BlockSpec((1, 1, D, ED), lambda i, be, bv: (layer, be[i], 0, 0)),
                  pl.BlockSpec((1, 1, ED, D), lambda i, be, bv: (layer, be[i], 0, 0))],
        out_specs=pl.BlockSpec(memory_space=pl.ANY),
        scratch_shapes=[pltpu.VMEM((D, ED), BF16), pltpu.VMEM((D, ED), BF16), pltpu.VMEM((ED, D), BF16),
                        buf, buf, buf, buf,
                        pltpu.SemaphoreType.DMA((2,)), pltpu.SemaphoreType.DMA(())],
    )
    return pl.pallas_call(
        _experts_kernel,
        out_shape=jax.ShapeDtypeStruct((out_rows * ROW_TILE, LANES), U32),
        grid_spec=grid_spec,
        compiler_params=_cparams(("arbitrary",)),
        name="experts",
    )(block_e, block_valid, tok3, tok3, dst3, hp, wg, wu, wd)


def _combine_kernel(*refs, final_norm):
    ys = refs[:TOP_K]
    wts_ref, hp_ref, x_ref, g2_ref, sg_ref, su_ref, sd_ref, fg_ref, o_ref = refs[TOP_K:]
    lo, hi = _unpack_pair(_load_rows(hp_ref))
    lo, hi = lo.astype(BF16), hi.astype(BF16)
    half = lo.shape[1]
    g = _dot(lo, sg_ref[:half, :]) + _dot(hi, sg_ref[half:, :])
    u = _dot(lo, su_ref[:half, :]) + _dot(hi, su_ref[half:, :])
    shared = _dot((g * _sigmoid(g) * u).astype(BF16), sd_ref[...])
    wts = wts_ref[...]
    acc_lo = shared[:, :half]
    acc_hi = shared[:, half:]
    for k in range(TOP_K):
        ylo, yhi = _unpack_pair(_load_rows(ys[k]))
        wk = wts[:, k:k + 1]
        acc_lo = acc_lo + wk * ylo
        acc_hi = acc_hi + wk * yhi
    g2 = g2_ref[0]
    x = x_ref[...]
    out_lo = x[:, :half] + g2[:, :half] * acc_lo
    out_hi = x[:, half:] + g2[:, half:] * acc_hi
    if final_norm:
        ms = (jnp.sum(out_lo * out_lo, axis=-1, keepdims=True)
              + jnp.sum(out_hi * out_hi, axis=-1, keepdims=True)) / (2 * half)
        inv = lax.rsqrt(ms + EPS)
        fg = fg_ref[...]
        out_lo = out_lo * inv * fg[:, :half]
        out_hi = out_hi * inv * fg[:, half:]
    o_ref[:, :half] = out_lo
    o_ref[:, half:] = out_hi


def _combine(o8, wts, hp, x1, g2, sg_bf, su_bf, sd_bf, fg, tm, tiles_per_batch, final_norm):
    T, D = x1.shape
    ED = sg_bf.shape[1]
    nt = T // tm
    kern = functools.partial(_combine_kernel, final_norm=final_norm)
    rows_blk = (tm * ROW_TILE, LANES)
    slot_specs = [pl.BlockSpec(rows_blk, functools.partial(lambda i, k: (k * nt + i, 0), k=k)) for k in range(TOP_K)]
    return pl.pallas_call(
        kern,
        out_shape=jax.ShapeDtypeStruct((T, D), F32),
        grid=(nt,),
        in_specs=slot_specs + [
            pl.BlockSpec((tm, LANES), lambda i: (i, 0)),
            pl.BlockSpec(rows_blk, lambda i: (i, 0)),
            pl.BlockSpec((tm, D), lambda i: (i, 0)),
            pl.BlockSpec((1, 1, D), lambda i: (i // tiles_per_batch, 0, 0)),
            pl.BlockSpec((D, ED), lambda i: (0, 0)),
            pl.BlockSpec((D, ED), lambda i: (0, 0)),
            pl.BlockSpec((ED, D), lambda i: (0, 0)),
            pl.BlockSpec((1, D), lambda i: (0, 0))],
        out_specs=pl.BlockSpec((tm, D), lambda i: (i, 0)),
        compiler_params=_cparams(("arbitrary",)),
        name="combine",
    )(*([o8] * TOP_K), wts, hp, x1, g2, sg_bf, su_bf, sd_bf, fg)


def _moe(hp, logits, x1, g2, n_out, tiles_per_batch, tm, router_b, layer, wg, wu, wd, sg, su, sd, fg,
         final_norm):
    T = hp.shape[0] // ROW_TILE
    R = EXPERT_ROWS
    E = N_EXPERTS
    A = T * TOP_K
    bias_pad = jnp.full((1, LANES), -1e30, F32).at[0, :E].set(router_b.astype(F32))
    idx, wts, counts = _route(logits, bias_pad, 256)
    counts = counts[0, :E].astype(I32)
    padded = (counts + R - 1) // R * R
    pend = jnp.cumsum(padded)
    n_blocks = -(-A // R) + E + 1
    total = n_blocks * R
    slot = jnp.arange(R, dtype=I32)[None, :]
    pad_keys = jnp.where(slot < (padded - counts)[:, None], jnp.arange(E, dtype=I32)[:, None], E)
    keys = jnp.concatenate([idx[:, :TOP_K].reshape(-1), pad_keys.reshape(-1),
                            jnp.full((total - A - E * R,), E, I32)])
    vals = jnp.concatenate([jnp.arange(A, dtype=I32), jnp.full((total - A,), -1, I32)])
    _, a_r = lax.sort((keys, vals), num_keys=1, is_stable=True)
    bstart = jnp.arange(n_blocks, dtype=I32) * R
    block_e = jnp.minimum(jnp.sum((pend[None, :] <= bstart[:, None]).astype(I32), axis=1), E - 1)
    block_valid = (bstart < pend[-1]).astype(I32)
    valid_r = a_r >= 0
    t_r = a_r // TOP_K
    tok_r = jnp.where(valid_r, t_r, 0)
    spare = TOP_K * n_out + jnp.arange(total, dtype=I32) % R
    dst_r = jnp.where(jnp.logical_and(valid_r, t_r < n_out), (a_r % TOP_K) * n_out + t_r, spare)
    first = lambda rows_idx: (rows_idx * ROW_TILE).reshape(n_blocks, 1, R)
    o8 = _experts(block_e, block_valid, first(tok_r), first(dst_r), hp, layer, wg, wu, wd, TOP_K * n_out + R)
    return _combine(o8, wts, hp, x1, g2, sg.astype(BF16), su.astype(BF16), sd.astype(BF16),
                    fg, tm, tiles_per_batch, final_norm)


def _rope_tables(n):
    rows = n // GRID_W
    row = jnp.repeat(jnp.arange(rows, dtype=F32), GRID_W)
    col = jnp.tile(jnp.arange(GRID_W, dtype=F32), rows)
    half = QK_DIM // 2
    inv = 1.0 / (ROPE_THETA ** (jnp.arange(0, half, 2, dtype=F32) / half))
    ang_r = row[:, None] * inv[None, :]
    ang_c = col[:, None] * inv[None, :]
    cr, sr, cc, sn = jnp.cos(ang_r), jnp.sin(ang_r), jnp.cos(ang_c), jnp.sin(ang_c)
    cos64 = jnp.concatenate([cr, cr, cc, cc], axis=1)
    sin64 = jnp.concatenate([-sr, sr, -sn, sn], axis=1)
    reps = LANES // QK_DIM
    return jnp.tile(cos64, (1, reps)), jnp.tile(sin64, (1, reps))


def kernel(x, c, ctx, c_ctx, ada_w, ada_b, norm1_g, norm2_g, w_in, w_out, gmlp_norm_g, gmlp_ws, gmlp_bs,
           lam_q1, lam_k1, lam_q2, lam_k2, subln_g, pool_w, pool_scale, router_w, router_b, exp_gate,
           exp_up, exp_down, sh_gate, sh_up, sh_down, final_g):
    B, N, D = x.shape
    NC = ctx.shape[1]
    depth = ada_w.shape[0]
    assert depth == 2 and B + 1 <= 8

    s8 = jnp.zeros((8, D), F32).at[:B].set(jax.nn.silu(c)).at[B].set(jax.nn.silu(c_ctx))
    mod = _ada_mod(s8, ada_w, ada_b).reshape(depth, 8, 6, D)

    def mods(layer, rows):
        m = mod[layer, rows]
        sh1, s1, g1, sh2, s2, g2 = (m[:, k][:, None, :] for k in range(6))
        return sh1, 1.0 + s1, g1, sh2, 1.0 + s2, g2

    rw_pad = [jnp.zeros((D, LANES), F32).at[:, :N_EXPERTS].set(router_w[l]) for l in range(depth)]
    row = lambda v: v.reshape(1, -1).astype(F32)

    sh1, sc1, g1, sh2, sc2, g2 = mods(0, slice(0, B))
    csh1, csc1, cg1, csh2, csc2, _ = (jnp.broadcast_to(v, (B, 1, D)) for v in mods(0, slice(B, B + 1)))
    lam_init = 0.8 - 0.6 * math.exp(-0.3 * 0)
    lam = (jnp.exp(jnp.sum(lam_q1[0] * lam_k1[0])) - jnp.exp(jnp.sum(lam_q2[0] * lam_k2[0]))
           + lam_init).reshape(1).astype(F32)
    w_in_bf = w_in[0].astype(BF16)
    cos_t, sin_t = _rope_tables(N)
    qscale = QK_DIM ** -0.5 * math.log2(math.e)
    tabs = (cos_t * qscale, sin_t * qscale, cos_t, sin_t)
    ones_c = jnp.ones((NC, LANES), F32)
    zeros_c = jnp.zeros((NC, LANES), F32)
    tabs_c = (ones_c * qscale, zeros_c, ones_c, zeros_c)

    p, vt = _proj_in(x, row(norm1_g[0]), sh1, sc1, w_in_bf, tabs, min(512, N))
    pc, vtc = _proj_in(ctx, row(norm1_g[0]), csh1, csc1, w_in_bf, tabs_c, min(512, NC))

    gw = gmlp_norm_g.shape[1]
    gd = gw // GMLP_GROUPS
    ws_bf = gmlp_ws[0].astype(BF16)
    bs_b = jnp.broadcast_to(gmlp_bs[0][:, :, None], (GMLP_GROUPS, CHUNK, gd)).astype(F32)
    gm = _gmlp(p, row(gmlp_norm_g[0]), ws_bf, bs_b, 256)
    gmc = _gmlp(pc, row(gmlp_norm_g[0]), ws_bf, bs_b, 256)

    g_b = jnp.broadcast_to(subln_g[0].astype(F32)[:, None], (V_DIM, LANES))
    q_col0 = 2 * gw // LANES
    k_col0 = q_col0 + HEADS * 2 * QK_DIM // LANES
    out_scale = 1.0 - lam_init
    o = _attention(lam, p, q_col0, k_col0, (p, vt), (pc, vtc), g_b, min(1024, N), out_scale)
    oc = _attention(lam, pc, q_col0, k_col0, None, (pc, vtc), g_b, NC, out_scale)

    w_out_bf = w_out[0].astype(BF16)
    x1, hp, lg = _proj_out(gm, o, w_out_bf, x, g1, row(norm2_g[0]), sh2, sc2, rw_pad[0], 256)
    _, hpc, lgc = _proj_out(gmc, oc, w_out_bf, ctx, cg1, row(norm2_g[0]), csh2, csc2, rw_pad[0], 256)

    T_lat = B * N
    hp_all = jnp.concatenate([hp.reshape(-1, LANES), hpc.reshape(-1, LANES)], axis=0)
    lg_all = jnp.concatenate([lg.reshape(T_lat, LANES), lgc.reshape(B * NC, LANES)], axis=0)
    tm_c = 128
    x = _moe(hp_all, lg_all, x1.reshape(T_lat, D), g2, T_lat, N // tm_c, tm_c, router_b[0],
             0, exp_gate, exp_up, exp_down, sh_gate[0], sh_up[0], sh_down[0],
             row(final_g), False).reshape(B, N, D)

    sh1, sc1, g1, sh2, sc2, g2 = mods(1, slice(0, B))
    x1, hp, lg = _pool_mixer(x, row(norm1_g[1]), sh1, sc1, pool_w[0].astype(BF16), row(pool_scale[0]),
                             g1, row(norm2_g[1]), sh2, sc2, rw_pad[1], 256)
    out = _moe(hp.reshape(-1, LANES), lg.reshape(T_lat, LANES), x1.reshape(T_lat, D), g2, T_lat,
               N // tm_c, tm_c, router_b[1], 1, exp_gate, exp_up, exp_down,
               sh_gate[1], sh_up[1], sh_down[1], row(final_g), True)
    return out.reshape(B, N, D)
```

```python
import functools
import math

import jax
import jax.numpy as jnp
from jax import lax
from jax.experimental import pallas as pl
from jax.experimental.pallas import tpu as pltpu

EPS = 1e-6
GRID_W = 64
ROPE_THETA = 10000.0
HEADS = 8
QK_DIM = 64
V_DIM = 128
V_ROWS = V_DIM + 16
GMLP_GROUPS = 4
CHUNK = 128
POOL_WINDOWS = (2, 4, 8, 16)
POOL_HALO = 8
N_EXPERTS = 64
TOP_K = 8
ROUTED_SCALE = 2.5
LANES = 128
EXPERT_ROWS = 256
ATTN_KEY_TILE = 1024
ROW_TILE = 8
GATHER_PRIORITY = 0
SCATTER_PRIORITY = 1
SAFE_SCORE_BOUND = 30.0
VMEM_LIMIT = 56 * 1024 * 1024

BF16 = jnp.bfloat16
F32 = jnp.float32
U32 = jnp.uint32
I32 = jnp.int32


def _cparams(sem):
    return pltpu.CompilerParams(dimension_semantics=sem, vmem_limit_bytes=VMEM_LIMIT)


def _dot(a, b):
    return jnp.dot(a, b, preferred_element_type=F32)


def _pack_pair(lo, hi):
    ulo = pltpu.bitcast(lo.astype(BF16).astype(F32), U32)
    uhi = pltpu.bitcast(hi.astype(BF16).astype(F32), U32)
    return (ulo >> 16) | uhi


def _unpack_pair(u):
    lo = pltpu.bitcast(u << 16, F32)
    hi = pltpu.bitcast(u & jnp.uint32(0xFFFF0000), F32)
    return lo, hi


def _store_rows(ref, rows):
    n = rows.shape[0]
    for j in range(ROW_TILE):
        ref[pl.ds(j, n, stride=ROW_TILE), :] = rows[:, j * LANES:(j + 1) * LANES]


def _load_rows(ref):
    n = ref.shape[0] // ROW_TILE
    return jnp.concatenate([ref[pl.ds(j, n, stride=ROW_TILE), :] for j in range(ROW_TILE)], axis=1)


def _rms_mod(x, g, shift, scale1p):
    y = x * lax.rsqrt(jnp.mean(x * x, axis=-1, keepdims=True) + EPS)
    return y * g * scale1p + shift


def _sigmoid(x):
    return 1.0 / (1.0 + jnp.exp(-x))


def _ada_kernel(s_ref, w_ref, b_ref, o_ref):
    o_ref[0] = _dot(s_ref[...].astype(BF16), w_ref[0].astype(BF16)) + b_ref[0]


def _ada_mod(s8, ada_w, ada_b):
    L, D, W = ada_w.shape
    tn = 768
    return pl.pallas_call(
        _ada_kernel,
        out_shape=jax.ShapeDtypeStruct((L, 8, W), F32),
        grid=(L, W // tn),
        in_specs=[pl.BlockSpec((8, D), lambda l, j: (0, 0)),
                  pl.BlockSpec((1, D, tn), lambda l, j: (l, 0, j)),
                  pl.BlockSpec((1, 1, tn), lambda l, j: (l, 0, j))],
        out_specs=pl.BlockSpec((1, 8, tn), lambda l, j: (l, 0, j)),
        compiler_params=_cparams(("arbitrary", "arbitrary")),
        name="ada_mod",
    )(s8, ada_w, ada_b.reshape(L, 1, W))


def _proj_in_kernel(x_ref, g_ref, sh_ref, sc_ref, w_ref, cq_ref, sq_ref, ck_ref, sk_ref,
                    o_ref, vt_ref, h_ref, *, q_tile, k_tile, v_tile):
    j = pl.program_id(2)

    @pl.when(j == 0)
    def _():
        h_ref[...] = _rms_mod(x_ref[0], g_ref[...], sh_ref[0], sc_ref[0]).astype(BF16)

    tm, tn = h_ref.shape[0], w_ref.shape[1]

    def project():
        return _dot(h_ref[...], w_ref[...])

    def rope(cos_ref, sin_ref):
        acc = project()
        reps = tn // LANES
        cos = jnp.concatenate([cos_ref[...]] * reps, axis=1)
        sin = jnp.concatenate([sin_ref[...]] * reps, axis=1)
        lane = lax.broadcasted_iota(I32, acc.shape, 1)
        partner = jnp.where((lane & 16) == 0,
                            pltpu.roll(acc, tn - 16, axis=1),
                            pltpu.roll(acc, 16, axis=1))
        return acc * cos + partner * sin

    is_q = j == q_tile
    is_k = j == k_tile
    is_v = j == v_tile

    @pl.when(is_q)
    def _():
        o_ref[0] = rope(cq_ref, sq_ref).astype(o_ref.dtype)

    @pl.when(is_k)
    def _():
        o_ref[0] = rope(ck_ref, sk_ref).astype(o_ref.dtype)

    @pl.when(jnp.logical_not(jnp.logical_or(jnp.logical_or(is_q, is_k), is_v)))
    def _():
        o_ref[0] = project().astype(o_ref.dtype)

    @pl.when(is_v)
    def _():
        acc = project()
        o_ref[0] = acc.astype(o_ref.dtype)
        ones = jnp.ones((V_ROWS - V_DIM, tm), vt_ref.dtype)
        for h in range(HEADS):
            vt_ref[0, h, 0:V_DIM, :] = acc[:, h * V_DIM:(h + 1) * V_DIM].T.astype(vt_ref.dtype)
            vt_ref[0, h, V_DIM:V_ROWS, :] = ones


def _proj_in(x, g, shift, scale1p, w_bf, tabs, tm):
    B, N, D = x.shape
    W = w_bf.shape[1]
    tn = 1024
    q_tile, k_tile, v_tile = 2, 3, 4
    assert W == 5 * tn and HEADS * V_DIM == tn
    kern = functools.partial(_proj_in_kernel, q_tile=q_tile, k_tile=k_tile, v_tile=v_tile)
    vec = pl.BlockSpec((1, 1, D), lambda b, i, j: (b, 0, 0))
    tab = pl.BlockSpec((tm, LANES), lambda b, i, j: (i, 0))
    return pl.pallas_call(
        kern,
        out_shape=(jax.ShapeDtypeStruct((B, N, W), BF16),
                   jax.ShapeDtypeStruct((B, HEADS, V_ROWS, N), BF16)),
        grid=(B, N // tm, W // tn),
        in_specs=[pl.BlockSpec((1, tm, D), lambda b, i, j: (b, i, 0)),
                  pl.BlockSpec((1, D), lambda b, i, j: (0, 0)),
                  vec, vec,
                  pl.BlockSpec((D, tn), lambda b, i, j: (0, j)),
                  tab, tab, tab, tab],
        out_specs=(pl.BlockSpec((1, tm, tn), lambda b, i, j: (b, i, j)),
                   pl.BlockSpec((1, HEADS, V_ROWS, tm), lambda b, i, j: (b, 0, 0, i))),
        scratch_shapes=[pltpu.VMEM((tm, D), BF16)],
        compiler_params=_cparams(("arbitrary", "arbitrary", "arbitrary")),
        name="proj_in",
    )(x, g, shift, scale1p, w_bf, *tabs)


def _gelu_tanh(x):
    return 0.5 * x * (1.0 + jnp.tanh(math.sqrt(2.0 / math.pi) * (x + 0.044715 * (x * x * x))))


def _gmlp_kernel(uv_ref, ng_ref, ws_ref, bs_ref, o_ref, *, chunks):
    gw = o_ref.shape[2]
    gd = gw // GMLP_GROUPS
    for c in range(chunks):
        rows = pl.ds(c * CHUNK, CHUNK)
        for g in range(GMLP_GROUPS):
            u = _gelu_tanh(uv_ref[0, rows, g * gd:(g + 1) * gd].astype(F32))
            v = _gelu_tanh(uv_ref[0, rows, gw + g * gd:gw + (g + 1) * gd].astype(F32))
            vn = v * lax.rsqrt(jnp.mean(v * v, axis=-1, keepdims=True) + EPS) * ng_ref[:, g * gd:(g + 1) * gd]
            mixed = _dot(ws_ref[g], vn.astype(BF16)) + bs_ref[g]
            o_ref[0, rows, g * gd:(g + 1) * gd] = (u * mixed).astype(o_ref.dtype)


def _gmlp(p, norm_g, ws_bf, bs_b, tg):
    B, N, _ = p.shape
    gw = norm_g.shape[1]
    gd = gw // GMLP_GROUPS
    kern = functools.partial(_gmlp_kernel, chunks=tg // CHUNK)
    return pl.pallas_call(
        kern,
        out_shape=jax.ShapeDtypeStruct((B, N, gw), BF16),
        grid=(B, N // tg),
        in_specs=[pl.BlockSpec((1, tg, 2 * gw), lambda b, i: (b, i, 0)),
                  pl.BlockSpec((1, gw), lambda b, i: (0, 0)),
                  pl.BlockSpec((GMLP_GROUPS, CHUNK, CHUNK), lambda b, i: (0, 0, 0)),
                  pl.BlockSpec((GMLP_GROUPS, CHUNK, gd), lambda b, i: (0, 0, 0))],
        out_specs=pl.BlockSpec((1, tg, gw), lambda b, i: (b, i, 0)),
        compiler_params=_cparams(("arbitrary", "arbitrary")),
        name="gmlp",
    )(p, norm_g, ws_bf, bs_b)


def _attn_kernel(*refs, tk, has_lat, out_scale):
    if has_lat:
        (lam_ref, q_ref, kl_ref, vl_ref, kc_ref, vc_ref, g_ref, o_ref,
         kmax_ref, acc1_ref, acc2_ref, pa1_ref, pa2_ref, pb1_ref, pb2_ref) = refs
        nl = kl_ref.shape[1] // tk
    else:
        (lam_ref, q_ref, kc_ref, vc_ref, g_ref, o_ref,
         kmax_ref, acc1_ref, acc2_ref, pa1_ref, pa2_ref, pb1_ref, pb2_ref) = refs
        nl = 0
    i = pl.program_id(2)
    tq = q_ref.shape[1]
    nc = kc_ref.shape[1]
    nt_dims = (((1,), (1,)), ((), ()))
    sub = lax.broadcasted_iota(I32, (8, LANES), 0)
    ln8 = lax.broadcasted_iota(I32, (8, LANES), 1)
    half_sel = jnp.where(jnp.logical_or(jnp.logical_and(sub == 0, ln8 < QK_DIM),
                                        jnp.logical_and(sub == 1, ln8 >= QK_DIM)), 1.0, 0.0).astype(BF16)

    def sq_norms(rows):
        rf = rows.astype(F32)
        return lax.dot_general(half_sel, (rf * rf).astype(BF16), nt_dims, preferred_element_type=F32)

    def lat_start(t):
        return t * tk if isinstance(t, int) else pl.multiple_of(t * tk, tk)

    def lat_k(t):
        return kl_ref[0, pl.ds(lat_start(t), tk), :]

    def lat_v(t):
        return vl_ref[0, 0, :, pl.ds(lat_start(t), tk)]

    def ctx_k():
        return kc_ref[0]

    def ctx_v():
        return vc_ref[0, 0]

    @pl.when(i == 0)
    def _():
        mx = jnp.max(sq_norms(ctx_k()), axis=1, keepdims=True)
        if nl:
            mx = lax.fori_loop(
                0, nl, lambda t, m: jnp.maximum(m, jnp.max(sq_norms(lat_k(t)), axis=1, keepdims=True)), mx)
        kmax_ref[...] = jnp.broadcast_to(mx, (8, LANES))

    q = q_ref[0]
    lane = lax.broadcasted_iota(I32, q.shape, 1)
    zero = jnp.zeros_like(q)
    qa = jnp.where(lane < QK_DIM, q, zero)
    qb = jnp.where(lane >= QK_DIM, q, zero)
    bound = 1.01 * jnp.sqrt(sq_norms(q) * kmax_ref[:, 0:1]) + 1e-3
    b1, b2 = bound[0:1], bound[1:2]
    safe = jnp.max(bound[0:2]) <= SAFE_SCORE_BOUND
    a0 = jnp.zeros((V_ROWS, tq), F32)

    @pl.when(safe)
    def _():
        bufs = ((pa1_ref, pa2_ref), (pb1_ref, pb2_ref))

        def qk(k, buf, rows):
            s1 = lax.dot_general(k, qa, nt_dims, preferred_element_type=F32)
            buf[0][0:rows, :] = jnp.exp2(s1 - b1).astype(BF16)
            s2 = lax.dot_general(k, qb, nt_dims, preferred_element_type=F32)
            buf[1][0:rows, :] = jnp.exp2(s2 - b2).astype(BF16)

        def pv(vt, buf, rows, a1, a2):
            a1 = a1 + _dot(vt, buf[0][0:rows, :])
            a2 = a2 + _dot(vt, buf[1][0:rows, :])
            return a1, a2

        if nl == 0:
            qk(ctx_k(), bufs[0], nc)
            a1, a2 = pv(ctx_v(), bufs[0], nc, a0, a0)
        else:
            qk(lat_k(0), bufs[0], tk)

            def pair(j, carry):
                a1, a2 = carry
                t = 2 * j
                qk(lat_k(t + 1), bufs[1], tk)
                a1, a2 = pv(lat_v(t), bufs[0], tk, a1, a2)
                qk(lat_k(t + 2), bufs[0], tk)
                return pv(lat_v(t + 1), bufs[1], tk, a1, a2)

            a1, a2 = lax.fori_loop(0, (nl - 1) // 2, pair, (a0, a0))
            if (nl - 1) % 2 == 1:
                qk(lat_k(nl - 1), bufs[1], tk)
                a1, a2 = pv(lat_v(nl - 2), bufs[0], tk, a1, a2)
                qk(ctx_k(), bufs[0], nc)
                a1, a2 = pv(lat_v(nl - 1), bufs[1], tk, a1, a2)
                a1, a2 = pv(ctx_v(), bufs[0], nc, a1, a2)
            else:
                qk(ctx_k(), bufs[1], nc)
                a1, a2 = pv(lat_v(nl - 1), bufs[0], tk, a1, a2)
                a1, a2 = pv(ctx_v(), bufs[1], nc, a1, a2)
        acc1_ref[...] = a1
        acc2_ref[...] = a2

    @pl.when(jnp.logical_not(safe))
    def _():
        def one_map(qm, k, vt, m, acc):
            s = lax.dot_general(k, qm, nt_dims, preferred_element_type=F32)
            m_new = jnp.maximum(m, jnp.max(s, axis=0, keepdims=True))
            alpha = jnp.exp2(m - m_new)
            p = jnp.exp2(s - m_new).astype(BF16)
            return m_new, alpha * acc + _dot(vt, p)

        def step(k, vt, carry):
            m1, a1, m2, a2 = carry
            m1, a1 = one_map(qa, k, vt, m1, a1)
            m2, a2 = one_map(qb, k, vt, m2, a2)
            return m1, a1, m2, a2

        m0 = jnp.full((1, tq), -jnp.inf, F32)
        carry = (m0, a0, m0, a0)
        if nl:
            carry = lax.fori_loop(0, nl, lambda t, c: step(lat_k(t), lat_v(t), c), carry)
        _, a1, _, a2 = step(ctx_k(), ctx_v(), carry)
        acc1_ref[...] = a1
        acc2_ref[...] = a2

    a1 = acc1_ref[...]
    a2 = acc2_ref[...]
    lam = lam_ref[0]
    o = a1[:V_DIM] / a1[V_DIM:V_DIM + 1] - lam * (a2[:V_DIM] / a2[V_DIM:V_DIM + 1])
    o = o * lax.rsqrt(jnp.mean(o * o, axis=0, keepdims=True) + EPS)
    reps = tq // LANES
    gain = jnp.concatenate([g_ref[...]] * reps, axis=1) if reps > 1 else g_ref[...]
    o_ref[0] = (o * gain * out_scale).T.astype(o_ref.dtype)


def _attention(lam, q_src, q_col0, k_col0, lat, ctx, g_b, tq, out_scale):
    B, nq, _ = q_src.shape
    pc, vtc = ctx
    nc = pc.shape[1]
    has_lat = lat is not None
    tk = ATTN_KEY_TILE
    assert nq % tq == 0
    head_q = pl.BlockSpec((1, tq, LANES), lambda b, h, i: (b, i, q_col0 + h))
    smem = pl.BlockSpec(memory_space=pltpu.SMEM)

    def kv_specs(n):
        return [pl.BlockSpec((1, n, LANES), lambda b, h, i: (b, 0, k_col0 + h)),
                pl.BlockSpec((1, 1, V_ROWS, n), lambda b, h, i: (b, h, 0, 0))]

    in_specs = [smem, head_q]
    args = [lam, q_src]
    if has_lat:
        p, vt = lat
        assert p.shape[1] % tk == 0
        in_specs += kv_specs(p.shape[1])
        args += [p, vt]
    in_specs += kv_specs(nc) + [pl.BlockSpec((V_DIM, LANES), lambda b, h, i: (0, 0))]
    args += [pc, vtc, g_b]
    prows = max(tk, nc) if has_lat else nc
    kern = functools.partial(_attn_kernel, tk=tk, has_lat=has_lat, out_scale=out_scale)
    return pl.pallas_call(
        kern,
        out_shape=jax.ShapeDtypeStruct((B, nq, HEADS * V_DIM), BF16),
        grid=(B, HEADS, nq // tq),
        in_specs=in_specs,
        out_specs=pl.BlockSpec((1, tq, LANES), lambda b, h, i: (b, i, h)),
        scratch_shapes=[pltpu.VMEM((8, LANES), F32),
                        pltpu.VMEM((V_ROWS, tq), F32),
                        pltpu.VMEM((V_ROWS, tq), F32)] + [pltpu.VMEM((prows, tq), BF16)] * 4,
        compiler_params=_cparams(("arbitrary", "arbitrary", "arbitrary")),
        name="diff_attn",
    )(*args)


def _mixer_epilogue(x, y, g1, n2g, sh2, sc2, rw, x1_ref, hp_ref, lg_ref):
    x1 = x + g1 * y
    x1_ref[0] = x1
    h2 = _rms_mod(x1, n2g, sh2, sc2)
    half = h2.shape[1] // 2
    _store_rows(hp_ref.at[0], _pack_pair(h2[:, :half], h2[:, half:]))
    lg_ref[0] = _dot(h2, rw)


def _proj_out_kernel(gm_ref, o_ref, w_ref, x_ref, g1_ref, n2g_ref, sh2_ref, sc2_ref, rw_ref,
                     x1_ref, hp_ref, lg_ref):
    gw = gm_ref.shape[2]
    y = _dot(gm_ref[0], w_ref[:gw, :]) + _dot(o_ref[0], w_ref[gw:, :])
    _mixer_epilogue(x_ref[0], y, g1_ref[0], n2g_ref[...], sh2_ref[0], sc2_ref[0], rw_ref[...],
                    x1_ref, hp_ref, lg_ref)


def _epilogue_out(B, N, D, tm):
    shapes = (jax.ShapeDtypeStruct((B, N, D), F32),
              jax.ShapeDtypeStruct((B, N * ROW_TILE, LANES), U32),
              jax.ShapeDtypeStruct((B, N, LANES), F32))
    specs = (pl.BlockSpec((1, tm, D), lambda b, i: (b, i, 0)),
             pl.BlockSpec((1, tm * ROW_TILE, LANES), lambda b, i: (b, i, 0)),
             pl.BlockSpec((1, tm, LANES), lambda b, i: (b, i, 0)))
    return shapes, specs


def _proj_out(gm, o, w_bf, x, g1, n2g, sh2, sc2, rw_pad, tm):
    B, N, D = x.shape
    gw = gm.shape[2]
    vec = pl.BlockSpec((1, 1, D), lambda b, i: (b, 0, 0))
    shapes, specs = _epilogue_out(B, N, D, tm)
    return pl.pallas_call(
        _proj_out_kernel,
        out_shape=shapes,
        grid=(B, N // tm),
        in_specs=[pl.BlockSpec((1, tm, gw), lambda b, i: (b, i, 0)),
                  pl.BlockSpec((1, tm, o.shape[2]), lambda b, i: (b, i, 0)),
                  pl.BlockSpec(w_bf.shape, lambda b, i: (0, 0)),
                  pl.BlockSpec((1, tm, D), lambda b, i: (b, i, 0)),
                  vec,
                  pl.BlockSpec((1, D), lambda b, i: (0, 0)),
                  vec, vec,
                  pl.BlockSpec((D, LANES), lambda b, i: (0, 0))],
        out_specs=specs,
        compiler_params=_cparams(("arbitrary", "arbitrary")),
        name="proj_out",
    )(gm, o, w_bf, x, g1, n2g, sh2, sc2, rw_pad)


def _pool_kernel(x_ref, xp_ref, xn_ref, n1g_ref, sh1_ref, sc1_ref, pw_ref, ps_ref,
                 g1_ref, n2g_ref, sh2_ref, sc2_ref, rw_ref, x1_ref, hp_ref, lg_ref, y_ref, *, seq):
    i = pl.program_id(1)
    last = pl.num_programs(1) - 1
    x = x_ref[0]
    tm, D = x.shape
    n1g, sh1, sc1 = n1g_ref[...], sh1_ref[0], sc1_ref[0]
    hc = _rms_mod(x, n1g, sh1, sc1)
    hp = _rms_mod(xp_ref[0], n1g, sh1, sc1) * (i > 0).astype(F32)
    hn = _rms_mod(xn_ref[0], n1g, sh1, sc1) * (i < last).astype(F32)
    he = jnp.concatenate([hp, hc, hn], axis=0).astype(BF16)
    groups = len(POOL_WINDOWS)
    gd = D // groups
    row = lax.broadcasted_iota(I32, (tm, tm + 2 * POOL_HALO), 0)
    col = lax.broadcasted_iota(I32, (tm, tm + 2 * POOL_HALO), 1)
    tglob = i * tm + lax.broadcasted_iota(I32, (tm, 1), 0)
    for g, w in enumerate(POOL_WINDOWS):
        lo = row + (POOL_HALO - w // 2)
        band = jnp.where(jnp.logical_and(col >= lo, col < lo + w), 1.0, 0.0).astype(BF16)
        s = _dot(band, he[:, g * gd:(g + 1) * gd])
        cnt = jnp.minimum(tglob + w // 2, seq) - jnp.maximum(tglob - w // 2, 0)
        pooled = s / cnt.astype(F32) - hc[:, g * gd:(g + 1) * gd]
        y_ref[:, g * gd:(g + 1) * gd] = _dot(pooled.astype(BF16), pw_ref[g])
    y = y_ref[...] * ps_ref[...]
    _mixer_epilogue(x, y, g1_ref[0], n2g_ref[...], sh2_ref[0], sc2_ref[0], rw_ref[...],
                    x1_ref, hp_ref, lg_ref)


def _pool_mixer(x, n1g, sh1, sc1, pw_bf, ps, g1, n2g, sh2, sc2, rw_pad, tm):
    B, N, D = x.shape
    gd = D // len(POOL_WINDOWS)
    hb = tm // POOL_HALO
    nhb = N // POOL_HALO
    vec = pl.BlockSpec((1, 1, D), lambda b, i: (b, 0, 0))
    row = pl.BlockSpec((1, D), lambda b, i: (0, 0))
    shapes, specs = _epilogue_out(B, N, D, tm)
    kern = functools.partial(_pool_kernel, seq=N)
    return pl.pallas_call(
        kern,
        out_shape=shapes,
        grid=(B, N // tm),
        in_specs=[pl.BlockSpec((1, tm, D), lambda b, i: (b, i, 0)),
                  pl.BlockSpec((1, POOL_HALO, D), lambda b, i: (b, jnp.maximum(i * hb - 1, 0), 0)),
                  pl.BlockSpec((1, POOL_HALO, D), lambda b, i: (b, jnp.minimum((i + 1) * hb, nhb - 1), 0)),
                  row, vec, vec,
                  pl.BlockSpec((len(POOL_WINDOWS), gd, gd), lambda b, i: (0, 0, 0)),
                  row, vec, row, vec, vec,
                  pl.BlockSpec((D, LANES), lambda b, i: (0, 0))],
        out_specs=specs,
        scratch_shapes=[pltpu.VMEM((tm, D), F32)],
        compiler_params=_cparams(("arbitrary", "arbitrary")),
        name="pool_mixer",
    )(x, x, x, n1g, sh1, sc1, pw_bf, ps, g1, n2g, sh2, sc2, rw_pad)


def _route_kernel(lg_ref, b_ref, idx_ref, w_ref, cnt_ref, carry_ref):
    @pl.when(pl.program_id(0) == 0)
    def _():
        carry_ref[...] = jnp.zeros_like(carry_ref)

    logits = lg_ref[...]
    tr = logits.shape[0]
    scores = _sigmoid(logits)
    sel = scores + b_ref[...]
    lane = lax.broadcasted_iota(I32, (tr, LANES), 1)
    picked = jnp.zeros((tr, LANES), F32)
    idx_o = jnp.zeros((tr, LANES), I32)
    w_o = jnp.zeros((tr, LANES), F32)
    wsum = jnp.zeros((tr, 1), F32)
    for k in range(TOP_K):
        mx = jnp.max(sel, axis=1, keepdims=True)
        first = jnp.min(jnp.where(sel == mx, lane, LANES), axis=1, keepdims=True)
        oh = lane == first
        wk = jnp.sum(jnp.where(oh, scores, 0.0), axis=1, keepdims=True)
        wsum = wsum + wk
        sel = jnp.where(oh, -jnp.inf, sel)
        picked = jnp.where(oh, 1.0, picked)
        idx_o = jnp.where(lane == k, first, idx_o)
        w_o = jnp.where(lane == k, wk, w_o)
    idx_ref[...] = idx_o
    w_ref[...] = w_o / wsum * ROUTED_SCALE
    carry_ref[...] = carry_ref[...] + jnp.sum(picked, axis=0, keepdims=True)
    cnt_ref[...] = carry_ref[...]


def _route(logits, bias_pad, tr):
    T = logits.shape[0]
    blk = pl.BlockSpec((tr, LANES), lambda i: (i, 0))
    one = pl.BlockSpec((1, LANES), lambda i: (0, 0))
    return pl.pallas_call(
        _route_kernel,
        out_shape=(jax.ShapeDtypeStruct((T, LANES), I32),
                   jax.ShapeDtypeStruct((T, LANES), F32),
                   jax.ShapeDtypeStruct((1, LANES), F32)),
        grid=(T // tr,),
        in_specs=[blk, one],
        out_specs=(blk, blk, one),
        scratch_shapes=[pltpu.VMEM((1, LANES), F32)],
        compiler_params=_cparams(("arbitrary",)),
        name="route",
    )(logits, bias_pad)


def _experts_kernel(be_ref, bv_ref, tok_next_ref, tok_first_ref, dst_prev_ref, hp_ref,
                    wg_ref, wu_ref, wd_ref, o8_ref, wgb, wub, wdb, xb0, xb1, yb0, yb1, gsem, ssem):
    i = pl.program_id(0)
    rows = xb0.shape[0] // ROW_TILE
    valid = bv_ref[i] != 0
    prev_valid = jnp.logical_and(i > 0, bv_ref[jnp.maximum(i - 1, 0)] != 0)
    fresh = jnp.logical_or(i == 0, be_ref[i] != be_ref[jnp.maximum(i - 1, 0)])
    xbufs, ybufs = (xb0, xb1), (yb0, yb1)

    def tile_at(ref, first):
        return ref.at[pl.ds(first, ROW_TILE), :]

    def gather(tok_ref, slot):
        return [pltpu.make_async_copy(tile_at(hp_ref, pl.multiple_of(tok_ref[0, 0, r], ROW_TILE)),
                                      tile_at(xbufs[slot], r * ROW_TILE), gsem.at[slot]) for r in range(rows)]

    def gather_wait(slot):
        for r in range(rows):
            pltpu.make_async_copy(tile_at(hp_ref, 0), tile_at(xbufs[slot], r * ROW_TILE), gsem.at[slot]).wait()

    def scatter(slot):
        return [pltpu.make_async_copy(tile_at(ybufs[slot], r * ROW_TILE),
                                      tile_at(o8_ref, pl.multiple_of(dst_prev_ref[0, 0, r], ROW_TILE)),
                                      ssem.at[slot]) for r in range(rows)]

    def scatter_wait(slot):
        for r in range(rows):
            pltpu.make_async_copy(tile_at(ybufs[slot], r * ROW_TILE), tile_at(o8_ref, 0), ssem.at[slot]).wait()

    def start_all(copies, priority):
        for cp in copies:
            cp.start(priority=priority)

    def swiglu(slot):
        lo, hi = _unpack_pair(_load_rows(xbufs[slot]))
        lo, hi = lo.astype(BF16), hi.astype(BF16)
        half = lo.shape[1]
        g = _dot(lo, wgb[:half, :]) + _dot(hi, wgb[half:, :])
        u = _dot(lo, wub[:half, :]) + _dot(hi, wub[half:, :])
        a = (g * _sigmoid(g) * u).astype(BF16)
        y = _dot(a, wdb[...])
        return _pack_pair(y[:, :half], y[:, half:])

    @pl.when(jnp.logical_and(valid, fresh))
    def _():
        wgb[...] = wg_ref[0, 0].astype(BF16)
        wub[...] = wu_ref[0, 0].astype(BF16)
        wdb[...] = wd_ref[0, 0].astype(BF16)

    @pl.when(jnp.logical_and(valid, i == 0))
    def _():
        start_all(gather(tok_first_ref, 0), GATHER_PRIORITY)
        gather_wait(0)
        start_all(gather(tok_next_ref, 1), GATHER_PRIORITY)
        _store_rows(ybufs[0], swiglu(0))
        gather_wait(1)

    def step(par, older_in_flight):
        start_all(gather(tok_next_ref, 1 - par), GATHER_PRIORITY)
        start_all(scatter(1 - par), SCATTER_PRIORITY)
        packed = swiglu(par)
        if older_in_flight:
            scatter_wait(par)
        _store_rows(ybufs[par], packed)
        gather_wait(1 - par)

    @pl.when(jnp.logical_and(valid, i == 1))
    def _():
        step(1, False)

    for par in (0, 1):
        @pl.when(jnp.logical_and(jnp.logical_and(valid, i > 1), i % 2 == par))
        def _():
            step(par, True)

        @pl.when(jnp.logical_and(jnp.logical_and(jnp.logical_not(valid), prev_valid), i % 2 == par))
        def _():
            start_all(scatter(1 - par), SCATTER_PRIORITY)

            @pl.when(i > 1)
            def _():
                scatter_wait(par)

            scatter_wait(1 - par)


def _experts(block_e, block_valid, tok3, dst3, hp, layer, wg, wu, wd, out_rows):
    nb, _, R = tok3.shape
    buf = pltpu.VMEM((R * ROW_TILE, LANES), U32)
    _, _, D, ED = wg.shape
    smem_blk = lambda f: pl.BlockSpec((1, 1, R), f, memory_space=pltpu.SMEM)
    grid_spec = pltpu.PrefetchScalarGridSpec(
        num_scalar_prefetch=2,
        grid=(nb,),
        in_specs=[smem_blk(lambda i, be, bv: (jnp.minimum(i + 1, nb - 1), 0, 0)),
                  smem_blk(lambda i, be, bv: (0, 0, 0)),
                  smem_blk(lambda i, be, bv: (jnp.maximum(i - 1, 0), 0, 0)),
                  pl.BlockSpec(memory_space=pl.ANY),
                  pl.BlockSpec((1, 1, D, ED), lambda i, be, bv: (layer, be[i], 0, 0)),
                  pl.BlockSpec((1, 1, D, ED), lambda i, be, bv: (layer, be[i], 0, 0)),
                  pl.BlockSpec((1, 1, ED, D), lambda i, be, bv: (layer, be[i], 0, 0))],
        out_specs=pl.BlockSpec(memory_space=pl.ANY),
        scratch_shapes=[pltpu.VMEM((D, ED), BF16), pltpu.VMEM((D, ED), BF16), pltpu.VMEM((ED, D), BF16),
                        buf, buf, buf, buf,
                        pltpu.SemaphoreType.DMA((2,)), pltpu.SemaphoreType.DMA((2,))],
    )
    return pl.pallas_call(
        _experts_kernel,
        out_shape=jax.ShapeDtypeStruct((out_rows * ROW_TILE, LANES), U32),
        grid_spec=grid_spec,
        compiler_params=_cparams(("arbitrary",)),
        name="experts",
    )(block_e, block_valid, tok3, tok3, dst3, hp, wg, wu, wd)


def _combine_kernel(*refs, final_norm):
    ys = refs[:TOP_K]
    wts_ref, hp_ref, x_ref, g2_ref, sg_ref, su_ref, sd_ref, fg_ref, o_ref = refs[TOP_K:]
    lo, hi = _unpack_pair(_load_rows(hp_ref))
    lo, hi = lo.astype(BF16), hi.astype(BF16)
    half = lo.shape[1]
    g = _dot(lo, sg_ref[:half, :]) + _dot(hi, sg_ref[half:, :])
    u = _dot(lo, su_ref[:half, :]) + _dot(hi, su_ref[half:, :])
    shared = _dot((g * _sigmoid(g) * u).astype(BF16), sd_ref[...])
    wts = wts_ref[...]
    acc_lo = shared[:, :half]
    acc_hi = shared[:, half:]
    for k in range(TOP_K):
        ylo, yhi = _unpack_pair(_load_rows(ys[k]))
        wk = wts[:, k:k + 1]
        acc_lo = acc_lo + wk * ylo
        acc_hi = acc_hi + wk * yhi
    g2 = g2_ref[0]
    x = x_ref[...]
    out_lo = x[:, :half] + g2[:, :half] * acc_lo
    out_hi = x[:, half:] + g2[:, half:] * acc_hi
    if final_norm:
        ms = (jnp.sum(out_lo * out_lo, axis=-1, keepdims=True)
              + jnp.sum(out_hi * out_hi, axis=-1, keepdims=True)) / (2 * half)
        inv = lax.rsqrt(ms + EPS)
        fg = fg_ref[...]
        out_lo = out_lo * inv * fg[:, :half]
        out_hi = out_hi * inv * fg[:, half:]
    o_ref[:, :half] = out_lo
    o_ref[:, half:] = out_hi


def _combine(o8, wts, hp, x1, g2, sg_bf, su_bf, sd_bf, fg, tm, tiles_per_batch, final_norm):
    T, D = x1.shape
    ED = sg_bf.shape[1]
    nt = T // tm
    kern = functools.partial(_combine_kernel, final_norm=final_norm)
    rows_blk = (tm * ROW_TILE, LANES)
    slot_specs = [pl.BlockSpec(rows_blk, functools.partial(lambda i, k: (k * nt + i, 0), k=k)) for k in range(TOP_K)]
    return pl.pallas_call(
        kern,
        out_shape=jax.ShapeDtypeStruct((T, D), F32),
        grid=(nt,),
        in_specs=slot_specs + [
            pl.BlockSpec((tm, LANES), lambda i: (i, 0)),
            pl.BlockSpec(rows_blk, lambda i: (i, 0)),
            pl.BlockSpec((tm, D), lambda i: (i, 0)),
            pl.BlockSpec((1, 1, D), lambda i: (i // tiles_per_batch, 0, 0)),
            pl.BlockSpec((D, ED), lambda i: (0, 0)),
            pl.BlockSpec((D, ED), lambda i: (0, 0)),
            pl.BlockSpec((ED, D), lambda i: (0, 0)),
            pl.BlockSpec((1, D), lambda i: (0, 0))],
        out_specs=pl.BlockSpec((tm, D), lambda i: (i, 0)),
        compiler_params=_cparams(("arbitrary",)),
        name="combine",
    )(*([o8] * TOP_K), wts, hp, x1, g2, sg_bf, su_bf, sd_bf, fg)


def _moe(hp, logits, x1, g2, n_out, tiles_per_batch, tm, router_b, layer, wg, wu, wd, sg, su, sd, fg,
         final_norm):
    T = hp.shape[0] // ROW_TILE
    R = EXPERT_ROWS
    E = N_EXPERTS
    A = T * TOP_K
    bias_pad = jnp.full((1, LANES), -1e30, F32).at[0, :E].set(router_b.astype(F32))
    idx, wts, counts = _route(logits, bias_pad, 256)
    counts = counts[0, :E].astype(I32)
    padded = (counts + R - 1) // R * R
    pend = jnp.cumsum(padded)
    n_blocks = -(-A // R) + E + 1
    total = n_blocks * R
    slot = jnp.arange(R, dtype=I32)[None, :]
    pad_keys = jnp.where(slot < (padded - counts)[:, None], jnp.arange(E, dtype=I32)[:, None], E)
    keys = jnp.concatenate([idx[:, :TOP_K].reshape(-1), pad_keys.reshape(-1),
                            jnp.full((total - A - E * R,), E, I32)])
    vals = jnp.concatenate([jnp.arange(A, dtype=I32), jnp.full((total - A,), -1, I32)])
    _, a_r = lax.sort((keys, vals), num_keys=1, is_stable=True)
    bstart = jnp.arange(n_blocks, dtype=I32) * R
    block_e = jnp.minimum(jnp.sum((pend[None, :] <= bstart[:, None]).astype(I32), axis=1), E - 1)
    block_valid = (bstart < pend[-1]).astype(I32)
    valid_r = a_r >= 0
    t_r = a_r // TOP_K
    tok_r = jnp.where(valid_r, t_r, 0)
    spare = TOP_K * n_out + jnp.arange(total, dtype=I32) % R
    dst_r = jnp.where(jnp.logical_and(valid_r, t_r < n_out), (a_r % TOP_K) * n_out + t_r, spare)
    first = lambda rows_idx: (rows_idx * ROW_TILE).reshape(n_blocks, 1, R)
    o8 = _experts(block_e, block_valid, first(tok_r), first(dst_r), hp, layer, wg, wu, wd, TOP_K * n_out + R)
    return _combine(o8, wts, hp, x1, g2, sg.astype(BF16), su.astype(BF16), sd.astype(BF16),
                    fg, tm, tiles_per_batch, final_norm)


def _rope_tables(n):
    rows = n // GRID_W
    row = jnp.repeat(jnp.arange(rows, dtype=F32), GRID_W)
    col = jnp.tile(jnp.arange(GRID_W, dtype=F32), rows)
    half = QK_DIM // 2
    inv = 1.0 / (ROPE_THETA ** (jnp.arange(0, half, 2, dtype=F32) / half))
    ang_r = row[:, None] * inv[None, :]
    ang_c = col[:, None] * inv[None, :]
    cr, sr, cc, sn = jnp.cos(ang_r), jnp.sin(ang_r), jnp.cos(ang_c), jnp.sin(ang_c)
    cos64 = jnp.concatenate([cr, cr, cc, cc], axis=1)
    sin64 = jnp.concatenate([-sr, sr, -sn, sn], axis=1)
    reps = LANES // QK_DIM
    return jnp.tile(cos64, (1, reps)), jnp.tile(sin64, (1, reps))


def kernel(x, c, ctx, c_ctx, ada_w, ada_b, norm1_g, norm2_g, w_in, w_out, gmlp_norm_g, gmlp_ws, gmlp_bs,
           lam_q1, lam_k1, lam_q2, lam_k2, subln_g, pool_w, pool_scale, router_w, router_b, exp_gate,
           exp_up, exp_down, sh_gate, sh_up, sh_down, final_g):
    B, N, D = x.shape
    NC = ctx.shape[1]
    depth = ada_w.shape[0]
    assert depth == 2 and B + 1 <= 8

    s8 = jnp.zeros((8, D), F32).at[:B].set(jax.nn.silu(c)).at[B].set(jax.nn.silu(c_ctx))
    mod = _ada_mod(s8, ada_w, ada_b).reshape(depth, 8, 6, D)

    def mods(layer, rows):
        m = mod[layer, rows]
        sh1, s1, g1, sh2, s2, g2 = (m[:, k][:, None, :] for k in range(6))
        return sh1, 1.0 + s1, g1, sh2, 1.0 + s2, g2

    rw_pad = [jnp.zeros((D, LANES), F32).at[:, :N_EXPERTS].set(router_w[l]) for l in range(depth)]
    row = lambda v: v.reshape(1, -1).astype(F32)

    sh1, sc1, g1, sh2, sc2, g2 = mods(0, slice(0, B))
    csh1, csc1, cg1, csh2, csc2, _ = (jnp.broadcast_to(v, (B, 1, D)) for v in mods(0, slice(B, B + 1)))
    lam_init = 0.8 - 0.6 * math.exp(-0.3 * 0)
    lam = (jnp.exp(jnp.sum(lam_q1[0] * lam_k1[0])) - jnp.exp(jnp.sum(lam_q2[0] * lam_k2[0]))
           + lam_init).reshape(1).astype(F32)
    w_in_bf = w_in[0].astype(BF16)
    cos_t, sin_t = _rope_tables(N)
    qscale = QK_DIM ** -0.5 * math.log2(math.e)
    tabs = (cos_t * qscale, sin_t * qscale, cos_t, sin_t)
    ones_c = jnp.ones((NC, LANES), F32)
    zeros_c = jnp.zeros((NC, LANES), F32)
    tabs_c = (ones_c * qscale, zeros_c, ones_c, zeros_c)

    p, vt = _proj_in(x, row(norm1_g[0]), sh1, sc1, w_in_bf, tabs, min(512, N))
    pc, vtc = _proj_in(ctx, row(norm1_g[0]), csh1, csc1, w_in_bf, tabs_c, min(512, NC))

    gw = gmlp_norm_g.shape[1]
    gd = gw // GMLP_GROUPS
    ws_bf = gmlp_ws[0].astype(BF16)
    bs_b = jnp.broadcast_to(gmlp_bs[0][:, :, None], (GMLP_GROUPS, CHUNK, gd)).astype(F32)
    gm = _gmlp(p, row(gmlp_norm_g[0]), ws_bf, bs_b, 256)
    gmc = _gmlp(pc, row(gmlp_norm_g[0]), ws_bf, bs_b, 256)

    g_b = jnp.broadcast_to(subln_g[0].astype(F32)[:, None], (V_DIM, LANES))
    q_col0 = 2 * gw // LANES
    k_col0 = q_col0 + HEADS * 2 * QK_DIM // LANES
    out_scale = 1.0 - lam_init
    o = _attention(lam, p, q_col0, k_col0, (p, vt), (pc, vtc), g_b, min(1024, N), out_scale)
    oc = _attention(lam, pc, q_col0, k_col0, None, (pc, vtc), g_b, NC, out_scale)

    w_out_bf = w_out[0].astype(BF16)
    x1, hp, lg = _proj_out(gm, o, w_out_bf, x, g1, row(norm2_g[0]), sh2, sc2, rw_pad[0], 256)
    _, hpc, lgc = _proj_out(gmc, oc, w_out_bf, ctx, cg1, row(norm2_g[0]), csh2, csc2, rw_pad[0], 256)

    T_lat = B * N
    hp_all = jnp.concatenate([hp.reshape(-1, LANES), hpc.reshape(-1, LANES)], axis=0)
    lg_all = jnp.concatenate([lg.reshape(T_lat, LANES), lgc.reshape(B * NC, LANES)], axis=0)
    tm_c = 128
    x = _moe(hp_all, lg_all, x1.reshape(T_lat, D), g2, T_lat, N // tm_c, tm_c, router_b[0],
             0, exp_gate, exp_up, exp_down, sh_gate[0], sh_up[0], sh_down[0],
             row(final_g), False).reshape(B, N, D)

    sh1, sc1, g1, sh2, sc2, g2 = mods(1, slice(0, B))
    x1, hp, lg = _pool_mixer(x, row(norm1_g[1]), sh1, sc1, pool_w[0].astype(BF16), row(pool_scale[0]),
                             g1, row(norm2_g[1]), sh2, sc2, rw_pad[1], 256)
    out = _moe(hp.reshape(-1, LANES), lg.reshape(T_lat, LANES), x1.reshape(T_lat, D), g2, T_lat,
               N // tm_c, tm_c, router_b[1], 1, exp_gate, exp_up, exp_down,
               sh_gate[1], sh_up[1], sh_down[1], row(final_g), True)
    return out.reshape(B, N, D)
```
